```python
import math
import jax, jax.numpy as jnp
from jax import lax
import numpy as np

D_MODEL = 1024
BATCH = 8
SEQ = 2048
DEPTH = 1

CHUNK = 64
N_META = 16
Q_BLOCK = 128
N_HEADS = 16
QK_NOPE = 64
QK_ROPE = 32
V_DIM = 64
Q_RANK = 256
KV_RANK = 128
ROPE_BASE = 10000.0
CONV_CH = D_MODEL
CONV_K = 31
D_FF = ((8 * D_MODEL + 3 * 256 - 1) // (3 * 256)) * 256
EPS = 1e-6
NEG_INF = -1e30

IN_Q = Q_RANK
IN_KV = KV_RANK
IN_KR = QK_ROPE
IN_GLU = 2 * CONV_CH
IN_GATE = 2 * D_MODEL
OFF_KV = IN_Q
OFF_KR = OFF_KV + IN_KV
OFF_GLU = OFF_KR + IN_KR
OFF_GATE = OFF_GLU + IN_GLU
N_IN = OFF_GATE + IN_GATE

kernel_name = "hybrid_mla_conformer_conv_swiglu_block"


def _rms_norm(x, g):
    xf = x.astype(jnp.float32)
    y = xf * lax.rsqrt(jnp.mean(xf * xf, axis=-1, keepdims=True) + EPS)
    return (y * g.astype(jnp.float32)).astype(x.dtype)


def _layer_norm(x, g, b):
    xf = x.astype(jnp.float32)
    mu = jnp.mean(xf, axis=-1, keepdims=True)
    xc = xf - mu
    var = jnp.mean(xc * xc, axis=-1, keepdims=True)
    y = xc * lax.rsqrt(var + EPS) * g.astype(jnp.float32) + b.astype(jnp.float32)
    return y.astype(x.dtype)


def _rope_tables(length):
    pos = jnp.arange(length, dtype=jnp.float32)
    inv = ROPE_BASE ** (-jnp.arange(0, QK_ROPE, 2, dtype=jnp.float32) / QK_ROPE)
    ang = pos[:, None] * inv[None, :]
    return jnp.cos(ang), jnp.sin(ang)


def _apply_rope(x, cos, sin):
    half = QK_ROPE // 2
    xf = x.astype(jnp.float32)
    x1, x2 = xf[..., :half], xf[..., half:]
    out = jnp.concatenate([x1 * cos - x2 * sin, x2 * cos + x1 * sin], axis=-1)
    return out.astype(x.dtype)


def _chunk_end(pos):
    if pos < N_META:
        return N_META
    return N_META + CHUNK * ((pos - N_META) // CHUNK + 1)


def _mla_branch(c_q, c_kv, k_r, q_norm_g, w_uq, kv_norm_g, w_ukv, w_attn_o):
    b, length, _ = c_q.shape
    q = _rms_norm(c_q, q_norm_g) @ w_uq
    q = q.reshape(b, length, N_HEADS, QK_NOPE + QK_ROPE)
    q_nope, q_rope = q[..., :QK_NOPE], q[..., QK_NOPE:]
    kv = _rms_norm(c_kv, kv_norm_g) @ w_ukv
    kv = kv.reshape(b, length, N_HEADS, QK_NOPE + V_DIM)
    k_nope, v = kv[..., :QK_NOPE], kv[..., QK_NOPE:]
    cos, sin = _rope_tables(length)
    q_rope = _apply_rope(q_rope, cos[None, :, None, :], sin[None, :, None, :])
    k_rope = _apply_rope(k_r, cos[None], sin[None])

    pos = jnp.arange(length)
    cid = jnp.where(pos < N_META, 0, 1 + (pos - N_META) // CHUNK)
    scale = 1.0 / math.sqrt(QK_NOPE + QK_ROPE)
    outs = []
    for s in range(0, length, Q_BLOCK):
        e = min(s + Q_BLOCK, length)
        kend = min(_chunk_end(e - 1), length)
        sc = (jnp.einsum('bqhd,bkhd->bhqk', q_nope[:, s:e], k_nope[:, :kend])
              + jnp.einsum('bqhr,bkr->bhqk', q_rope[:, s:e], k_rope[:, :kend]))
        sc = sc.astype(jnp.float32) * scale
        mask = cid[None, :kend] <= cid[s:e, None]
        sc = jnp.where(mask[None, None], sc, NEG_INF)
        p = jax.nn.softmax(sc, axis=-1).astype(v.dtype)
        outs.append(jnp.einsum('bhqk,bkhd->bqhd', p, v[:, :kend]))
    o = jnp.concatenate(outs, axis=1).reshape(b, length, N_HEADS * V_DIM)
    return o @ w_attn_o


def _conv_branch(glu_in, conv_w, conv_b, conv_ln_g, conv_ln_b, w_conv_o):
    a, g = glu_in[..., :CONV_CH], glu_in[..., CONV_CH:]
    u = a * jax.nn.sigmoid(g)
    y = lax.conv_general_dilated(
        u, conv_w[:, None, :].astype(u.dtype), window_strides=(1,),
        padding=[(CONV_K - 1, 0)], dimension_numbers=('NWC', 'WIO', 'NWC'),
        feature_group_count=CONV_CH)
    y = y + conv_b
    y = jax.nn.silu(_layer_norm(y, conv_ln_g, conv_ln_b))
    return y @ w_conv_o


def _layer(x, mix_pre_g, w_in, q_norm_g, w_uq, kv_norm_g, w_ukv, w_attn_o,
           conv_w, conv_b, conv_ln_g, conv_ln_b, w_conv_o, w_out, mix_post_g,
           ffn_pre_g, w_ffn_in, w_ffn_out, ffn_post_g):
    h = _rms_norm(x, mix_pre_g)
    proj = h @ w_in
    c_q = proj[..., :OFF_KV]
    c_kv = proj[..., OFF_KV:OFF_KR]
    k_r = proj[..., OFF_KR:OFF_GLU]
    glu_in = proj[..., OFF_GLU:OFF_GATE]
    g_a = proj[..., OFF_GATE:OFF_GATE + D_MODEL]
    g_b = proj[..., OFF_GATE + D_MODEL:]
    y_a = _mla_branch(c_q, c_kv, k_r, q_norm_g, w_uq, kv_norm_g, w_ukv, w_attn_o)
    y_b = _conv_branch(glu_in, conv_w, conv_b, conv_ln_g, conv_ln_b, w_conv_o)
    merged = jax.nn.sigmoid(g_a) * y_a + jax.nn.sigmoid(g_b) * y_b
    x = x + _rms_norm(merged @ w_out, mix_post_g)
    h = _rms_norm(x, ffn_pre_g)
    gu = h @ w_ffn_in
    f = (jax.nn.silu(gu[..., :D_FF]) * gu[..., D_FF:]) @ w_ffn_out
    return x + _rms_norm(f, ffn_post_g)


def setup_inputs(seed: int = 0) -> dict:
    key = jax.random.key(seed)
    ks = jax.random.split(key, 24)
    f32 = jnp.float32

    def w(k, shape, fan_in):
        return jax.random.normal(k, shape, f32) * (fan_in ** -0.5)

    def gain(k, n):
        return 1.0 + 0.02 * jax.random.normal(k, (DEPTH, n), f32)

    def bias(k, n):
        return 0.02 * jax.random.normal(k, (DEPTH, n), f32)

    return {
        "x": jax.random.normal(ks[0], (BATCH, SEQ, D_MODEL), f32),
        "meta": jax.random.normal(ks[1], (N_META, D_MODEL), f32),
        "mix_pre_g": gain(ks[2], D_MODEL),
        "w_in": w(ks[3], (DEPTH, D_MODEL, N_IN), D_MODEL),
        "q_norm_g": gain(ks[4], Q_RANK),
        "w_uq": w(ks[5], (DEPTH, Q_RANK, N_HEADS * (QK_NOPE + QK_ROPE)), Q_RANK),
        "kv_norm_g": gain(ks[6], KV_RANK),
        "w_ukv": w(ks[7], (DEPTH, KV_RANK, N_HEADS * (QK_NOPE + V_DIM)), KV_RANK),
        "w_attn_o": w(ks[8], (DEPTH, N_HEADS * V_DIM, D_MODEL), N_HEADS * V_DIM),
        "conv_w": w(ks[9], (DEPTH, CONV_K, CONV_CH), CONV_K),
        "conv_b": bias(ks[10], CONV_CH),
        "conv_ln_g": gain(ks[11], CONV_CH),
        "conv_ln_b": bias(ks[12], CONV_CH),
        "w_conv_o": w(ks[13], (DEPTH, CONV_CH, D_MODEL), CONV_CH),
        "w_out": w(ks[14], (DEPTH, D_MODEL, D_MODEL), D_MODEL),
        "mix_post_g": gain(ks[15], D_MODEL),
        "ffn_pre_g": gain(ks[16], D_MODEL),
        "w_ffn_in": w(ks[17], (DEPTH, D_MODEL, 2 * D_FF), D_MODEL),
        "w_ffn_out": w(ks[18], (DEPTH, D_FF, D_MODEL), D_FF),
        "ffn_post_g": gain(ks[19], D_MODEL),
    }


def reference(x, meta, mix_pre_g, w_in, q_norm_g, w_uq, kv_norm_g, w_ukv,
              w_attn_o, conv_w, conv_b, conv_ln_g, conv_ln_b, w_conv_o, w_out,
              mix_post_g, ffn_pre_g, w_ffn_in, w_ffn_out, ffn_post_g):
    b = x.shape[0]
    m = jnp.broadcast_to(meta.astype(x.dtype)[None], (b, N_META, D_MODEL))
    h = jnp.concatenate([m, x], axis=1)
    for l in range(DEPTH):
        h = _layer(h, mix_pre_g[l], w_in[l], q_norm_g[l], w_uq[l], kv_norm_g[l],
                   w_ukv[l], w_attn_o[l], conv_w[l], conv_b[l], conv_ln_g[l],
                   conv_ln_b[l], w_conv_o[l], w_out[l], mix_post_g[l],
                   ffn_pre_g[l], w_ffn_in[l], w_ffn_out[l], ffn_post_g[l])
    return h[:, N_META:]
```

```python
import functools
import math

import jax
import jax.numpy as jnp
from jax import lax
from jax.experimental import pallas as pl
from jax.experimental.pallas import tpu as pltpu

CHUNK = 64
N_META = 16
N_HEADS = 16
QK_NOPE = 64
QK_ROPE = 32
V_DIM = 64
Q_RANK = 256
KV_RANK = 128
ROPE_BASE = 10000.0
CONV_K = 31
EPS = 1e-6
NEG_INF = -1e30

LANES = 128
VMEM_LIMIT = 56 * 1024 * 1024

HEAD_W = LANES
HALF_ROPE = QK_ROPE // 2

BF16 = jnp.bfloat16
F32 = jnp.float32


def _rms(x, g):
    ms = jnp.mean(x * x, axis=-1, keepdims=True)
    return x * lax.rsqrt(ms + EPS) * g


def _sigmoid(x):
    return 0.5 * (jnp.tanh(0.5 * x) + 1.0)


def _dot(a, b):
    return jnp.dot(a, b, preferred_element_type=F32)


def _dot_t(a, b):
    return lax.dot_general(a, b, (((1,), (1,)), ((), ())),
                           preferred_element_type=F32)


def _rope(g, c, s_lo, s_hi):
    up = pltpu.roll(g, LANES - HALF_ROPE, 1)
    dn = pltpu.roll(g, HALF_ROPE, 1)
    return g * c + up * s_lo + dn * s_hi


def _inproj_kernel(x_ref, tab_ref, gpre_ref, wsm_ref, wbig_ref, gq_ref, wq_ref,
                   gkv_ref, wk_ref, wv_ref,
                   q_ref, k_ref, v_ref, u_ref, sga_ref, sgb_ref):
    d = x_ref.shape[1]
    xb = _rms(x_ref[...], gpre_ref[...]).astype(BF16)

    small = _dot(xb, wsm_ref[...])
    qn = _rms(small[:, :Q_RANK], gq_ref[...]).astype(BF16)
    kvn = _rms(small[:, Q_RANK:Q_RANK + KV_RANK], gkv_ref[...]).astype(BF16)
    kr = _rope(small[:, Q_RANK + KV_RANK:], tab_ref[3], tab_ref[4], tab_ref[5])

    kin = jnp.concatenate([kvn, kr.astype(BF16)], axis=1)
    k_ref[...] = _dot(kin, wk_ref[...]).astype(BF16)
    v_ref[...] = _dot(kvn, wv_ref[...]).astype(BF16)

    q = _dot(qn, wq_ref[...])
    cq, sq_lo, sq_hi = tab_ref[0], tab_ref[1], tab_ref[2]
    for h in range(N_HEADS):
        sl = slice(h * HEAD_W, (h + 1) * HEAD_W)
        q_ref[:, sl] = _rope(q[:, sl], cq, sq_lo, sq_hi).astype(BF16)

    a = _dot(xb, wbig_ref[:, 0:d])
    g = _dot(xb, wbig_ref[:, d:2 * d])
    u_ref[...] = (a * _sigmoid(g)).astype(BF16)
    sga_ref[...] = _sigmoid(_dot(xb, wbig_ref[:, 2 * d:3 * d])).astype(BF16)
    sgb_ref[...] = _sigmoid(_dot(xb, wbig_ref[:, 3 * d:4 * d])).astype(BF16)


def _const_spec(shape):
    nd = len(shape)
    return pl.BlockSpec(shape, lambda *_: (0,) * nd, pipeline_mode=pl.Buffered(1))


def _inproj(x2d, tabs, weights, tm, tiles_per_seq):
    rows, d = x2d.shape
    gpre, wsm, wbig, gq, wq, gkv, wk, wv = weights
    n_q = wq.shape[1]
    n_v = wv.shape[1]
    row_spec = lambda w: pl.BlockSpec((tm, w), lambda i: (i, 0))
    out_shapes = [
        jax.ShapeDtypeStruct((rows, n_q), BF16),
        jax.ShapeDtypeStruct((rows, n_q), BF16),
        jax.ShapeDtypeStruct((rows, n_v), BF16),
        jax.ShapeDtypeStruct((rows, d), BF16),
        jax.ShapeDtypeStruct((rows, d), BF16),
        jax.ShapeDtypeStruct((rows, d), BF16),
    ]
    return pl.pallas_call(
        _inproj_kernel,
        grid=(rows // tm,),
        in_specs=[
            row_spec(d),
            pl.BlockSpec((6, tm, LANES), lambda i: (0, i % tiles_per_seq, 0)),
            _const_spec(gpre.shape), _const_spec(wsm.shape),
            _const_spec(wbig.shape), _const_spec(gq.shape),
            _const_spec(wq.shape), _const_spec(gkv.shape),
            _const_spec(wk.shape), _const_spec(wv.shape),
        ],
        out_specs=[row_spec(n_q), row_spec(n_q), row_spec(n_v),
                   row_spec(d), row_spec(d), row_spec(d)],
        out_shape=out_shapes,
        compiler_params=pltpu.CompilerParams(
            dimension_semantics=("parallel",), vmem_limit_bytes=VMEM_LIMIT),
        name="inproj",
    )(x2d, tabs, gpre, wsm, wbig, gq, wq, gkv, wk, wv)


def _attn_kernel(q_ref, k_ref, v_ref, km_ref, vm_ref, o_ref, *, tq):
    i = pl.program_id(2)
    heads = (0, 1)
    qs = [q_ref[:, h * HEAD_W:(h + 1) * HEAD_W] for h in heads]

    vm = vm_ref[...]
    ms, ls, accs = [], [], []
    for h in heads:
        s = _dot_t(qs[h], km_ref[:, h * HEAD_W:(h + 1) * HEAD_W])
        m = jnp.max(s, axis=1, keepdims=True)
        p = jnp.exp(s - m)
        ms.append(m)
        ls.append(jnp.sum(p, axis=1, keepdims=True))
        accs.append(_dot(p.astype(BF16), vm))

    def tile(j, carry, mask):
        off = pl.multiple_of(j * tq, tq)
        vt = v_ref[pl.ds(off, tq), :]
        out = []
        for h in heads:
            m, l, acc = carry[3 * h:3 * h + 3]
            kt = k_ref[pl.ds(off, tq), h * HEAD_W:(h + 1) * HEAD_W]
            s = _dot_t(qs[h], kt)
            if mask is not None:
                s = jnp.where(mask, s, NEG_INF)
            m_new = jnp.maximum(m, jnp.max(s, axis=1, keepdims=True))
            alpha = jnp.exp(m - m_new)
            p = jnp.exp(s - m_new)
            l = alpha * l + jnp.sum(p, axis=1, keepdims=True)
            acc = alpha * acc + _dot(p.astype(BF16), vt)
            out += [m_new, l, acc]
        return tuple(out)

    carry = (ms[0], ls[0], accs[0], ms[1], ls[1], accs[1])
    carry = lax.fori_loop(0, i, lambda j, c: tile(j, c, None), carry)
    rc = lax.broadcasted_iota(jnp.int32, (tq, tq), 0) // CHUNK
    cc = lax.broadcasted_iota(jnp.int32, (tq, tq), 1) // CHUNK
    carry = tile(i, carry, cc <= rc)

    o0 = carry[2] / carry[1]
    o1 = carry[5] / carry[4]
    lane = lax.broadcasted_iota(jnp.int32, o0.shape, 1)
    o_ref[...] = jnp.where(lane < V_DIM, o0, o1).astype(o_ref.dtype)


def _attention(q, k, v, km, vm, tq):
    b, s, _ = q.shape
    n_pairs = N_HEADS // 2
    return pl.pallas_call(
        functools.partial(_attn_kernel, tq=tq),
        grid=(b, n_pairs, s // tq),
        in_specs=[
            pl.BlockSpec((None, tq, 2 * HEAD_W), lambda bi, hp, i: (bi, i, hp)),
            pl.BlockSpec((None, s, 2 * HEAD_W), lambda bi, hp, i: (bi, 0, hp)),
            pl.BlockSpec((None, s, 2 * V_DIM), lambda bi, hp, i: (bi, 0, hp)),
            pl.BlockSpec((N_META, 2 * HEAD_W), lambda bi, hp, i: (0, hp)),
            pl.BlockSpec((N_META, 2 * V_DIM), lambda bi, hp, i: (0, hp)),
        ],
        out_specs=pl.BlockSpec((None, tq, 2 * V_DIM), lambda bi, hp, i: (bi, i, hp)),
        out_shape=jax.ShapeDtypeStruct((b, s, N_HEADS * V_DIM), BF16),
        compiler_params=pltpu.CompilerParams(
            dimension_semantics=("parallel", "parallel", "arbitrary"),
            vmem_limit_bytes=VMEM_LIMIT),
        name="attention",
    )(q, k, v, km, vm)


CONV_PAD = 32
CONV_RB = 64


def _conv_kernel(u_ref, um_ref, w_ref, b_ref, y_ref, buf_ref):
    s = u_ref.shape[0]
    buf_ref[0:CONV_PAD - N_META, :] = jnp.zeros((CONV_PAD - N_META, LANES), F32)
    buf_ref[CONV_PAD - N_META:CONV_PAD, :] = um_ref[...].astype(F32)
    buf_ref[CONV_PAD:, :] = u_ref[...].astype(F32)
    base = CONV_PAD - (CONV_K - 1)
    bias = b_ref[...]

    def block(rb, _):
        r0 = pl.multiple_of(rb * CONV_RB, CONV_RB)
        acc = jnp.broadcast_to(bias, (CONV_RB, LANES))
        for kk in range(CONV_K):
            acc = acc + w_ref[kk:kk + 1, :] * buf_ref[pl.ds(r0 + base + kk, CONV_RB), :]
        y_ref[pl.ds(r0, CONV_RB), :] = acc.astype(y_ref.dtype)
        return 0

    lax.fori_loop(0, s // CONV_RB, block, 0)


def _conv(u, um, conv_w, conv_b):
    b, s, c = u.shape
    return pl.pallas_call(
        _conv_kernel,
        grid=(b, c // LANES),
        in_specs=[
            pl.BlockSpec((None, s, LANES), lambda bi, ci: (bi, 0, ci)),
            pl.BlockSpec((N_META, LANES), lambda bi, ci: (0, ci)),
            pl.BlockSpec((CONV_K, LANES), lambda bi, ci: (0, ci)),
            pl.BlockSpec((1, LANES), lambda bi, ci: (0, ci)),
        ],
        out_specs=pl.BlockSpec((None, s, LANES), lambda bi, ci: (bi, 0, ci)),
        out_shape=jax.ShapeDtypeStruct((b, s, c), BF16),
        scratch_shapes=[pltpu.VMEM((s + CONV_PAD, LANES), F32)],
        compiler_params=pltpu.CompilerParams(
            dimension_semantics=("parallel", "parallel"),
            vmem_limit_bytes=VMEM_LIMIT),
        name="dwconv",
    )(u, um, conv_w, conv_b)


FF_CHUNK = 256


def _tail_kernel(x_ref, o_ref, yc_ref, sga_ref, sgb_ref,
                 lng_ref, lnb_ref, wao_ref, wco_ref, wout_ref, gpost_ref,
                 gfpre_ref, wfin_ref, wfout_ref, gfpost_ref,
                 out_ref, act_ref):
    d_ff = wfout_ref.shape[0]

    yc = yc_ref[...].astype(F32)
    mu = jnp.mean(yc, axis=-1, keepdims=True)
    xc = yc - mu
    var = jnp.mean(xc * xc, axis=-1, keepdims=True)
    z = xc * lax.rsqrt(var + EPS) * lng_ref[...] + lnb_ref[...]
    z = (z * _sigmoid(z)).astype(BF16)

    y_a = _dot(o_ref[...], wao_ref[...])
    y_b = _dot(z, wco_ref[...])
    merged = sga_ref[...].astype(F32) * y_a + sgb_ref[...].astype(F32) * y_b
    mo = _dot(merged.astype(BF16), wout_ref[...])
    x1 = x_ref[...] + _rms(mo, gpost_ref[...])

    hb = _rms(x1, gfpre_ref[...]).astype(BF16)
    for c0 in range(0, d_ff, FF_CHUNK):
        g = _dot(hb, wfin_ref[:, c0:c0 + FF_CHUNK])
        up = _dot(hb, wfin_ref[:, d_ff + c0:d_ff + c0 + FF_CHUNK])
        act_ref[:, c0:c0 + FF_CHUNK] = (g * _sigmoid(g) * up).astype(BF16)
    f = _dot(act_ref[...], wfout_ref[...])
    out_ref[...] = x1 + _rms(f, gfpost_ref[...])


def _tail(x2d, o2d, yc2d, sga, sgb, weights, tm):
    rows, d = x2d.shape
    d_ff = weights[8].shape[0]
    row_spec = pl.BlockSpec((tm, d), lambda i: (i, 0))
    return pl.pallas_call(
        _tail_kernel,
        grid=(rows // tm,),
        in_specs=[row_spec] * 5 + [_const_spec(w.shape) for w in weights],
        out_specs=row_spec,
        out_shape=jax.ShapeDtypeStruct((rows, d), F32),
        scratch_shapes=[pltpu.VMEM((tm, d_ff), BF16)],
        compiler_params=pltpu.CompilerParams(
            dimension_semantics=("parallel",), vmem_limit_bytes=VMEM_LIMIT),
        name="tail",
    )(x2d, o2d, yc2d, sga, sgb, *weights)


def _rope_tables(length):
    pos = jnp.arange(length, dtype=F32)
    inv = ROPE_BASE ** (-jnp.arange(0, QK_ROPE, 2, dtype=F32) / QK_ROPE)
    ang = pos[:, None] * inv[None, :]
    cos, sin = jnp.cos(ang), jnp.sin(ang)
    zeros = lambda w: jnp.zeros((length, w), F32)
    ones = lambda w: jnp.ones((length, w), F32)
    scale = 1.0 / math.sqrt(QK_NOPE + QK_ROPE)
    tail = HEAD_W - QK_NOPE - QK_ROPE
    cq = jnp.concatenate([ones(QK_NOPE), cos, cos, zeros(tail)], axis=1) * scale
    sq_lo = jnp.concatenate([zeros(QK_NOPE), -sin, zeros(HALF_ROPE + tail)], axis=1) * scale
    sq_hi = jnp.concatenate([zeros(QK_NOPE + HALF_ROPE), sin, zeros(tail)], axis=1) * scale
    ktail = LANES - QK_ROPE
    ck = jnp.concatenate([cos, cos, zeros(ktail)], axis=1)
    sk_lo = jnp.concatenate([-sin, zeros(HALF_ROPE + ktail)], axis=1)
    sk_hi = jnp.concatenate([zeros(HALF_ROPE), sin, zeros(ktail)], axis=1)
    return jnp.stack([cq, sq_lo, sq_hi, ck, sk_lo, sk_hi])


def kernel(x, meta, mix_pre_g, w_in, q_norm_g, w_uq, kv_norm_g, w_ukv, w_attn_o,
           conv_w, conv_b, conv_ln_g, conv_ln_b, w_conv_o, w_out, mix_post_g,
           ffn_pre_g, w_ffn_in, w_ffn_out, ffn_post_g):
    assert w_in.shape[0] == 1, "single-layer block"
    b, s, d = x.shape
    row = lambda g: g[0][None, :].astype(F32)

    w_in0 = w_in[0]
    off_glu = Q_RANK + KV_RANK + QK_ROPE
    wsm = jnp.pad(w_in0[:, :off_glu], ((0, 0), (0, 4 * LANES - off_glu))).astype(BF16)
    wbig = w_in0[:, off_glu:].astype(BF16)

    qk_dim = QK_NOPE + QK_ROPE
    wq = w_uq[0].reshape(Q_RANK, N_HEADS, qk_dim)
    wq = jnp.pad(wq, ((0, 0), (0, 0), (0, HEAD_W - qk_dim)))
    wq = wq.reshape(Q_RANK, N_HEADS * HEAD_W).astype(BF16)

    wkv = w_ukv[0].reshape(KV_RANK, N_HEADS, QK_NOPE + V_DIM)
    wk_nope = jnp.pad(wkv[:, :, :QK_NOPE], ((0, 0), (0, 0), (0, HEAD_W - QK_NOPE)))
    place = jnp.zeros((LANES, N_HEADS, HEAD_W), F32)
    jj = jnp.arange(QK_ROPE)
    place = place.at[jj, :, QK_NOPE + jj].set(1.0)
    wk = jnp.concatenate([wk_nope, place], axis=0)
    wk = wk.reshape(KV_RANK + LANES, N_HEADS * HEAD_W).astype(BF16)
    wv = wkv[:, :, QK_NOPE:].reshape(KV_RANK, N_HEADS * V_DIM).astype(BF16)

    in_weights = (row(mix_pre_g), wsm, wbig, row(q_norm_g), wq, row(kv_norm_g), wk, wv)

    tabs = _rope_tables(N_META + s)
    tabs_meta, tabs_real = tabs[:, :N_META], tabs[:, N_META:]

    tm = 256
    x2d = x.reshape(b * s, d)
    q, k, v, u, sga, sgb = _inproj(x2d, tabs_real, in_weights, tm, s // tm)
    _, km, vm, um, _, _ = _inproj(meta.astype(F32), tabs_meta, in_weights, N_META, 1)

    o = _attention(q.reshape(b, s, -1), k.reshape(b, s, -1), v.reshape(b, s, -1),
                   km, vm, tq=256)

    yc = _conv(u.reshape(b, s, d), um, conv_w[0].astype(F32), row(conv_b))

    tail_weights = (row(conv_ln_g), row(conv_ln_b), w_attn_o[0].astype(BF16),
                    w_conv_o[0].astype(BF16), w_out[0].astype(BF16), row(mix_post_g),
                    row(ffn_pre_g), w_ffn_in[0].astype(BF16), w_ffn_out[0].astype(BF16),
                    row(ffn_post_g))
    out = _tail(x2d, o.reshape(b * s, -1), yc.reshape(b * s, d), sga, sgb,
                tail_weights, tm)
    return out.reshape(b, s, d)
```

```python
import functools
import math

import jax
import jax.numpy as jnp
from jax import lax
from jax.experimental import pallas as pl
from jax.experimental.pallas import tpu as pltpu

CHUNK = 64
N_META = 16
N_HEADS = 16
QK_NOPE = 64
QK_ROPE = 32
V_DIM = 64
Q_RANK = 256
KV_RANK = 128
ROPE_BASE = 10000.0
CONV_K = 31
EPS = 1e-6
NEG_INF = -1e30

LANES = 128
VMEM_LIMIT = 56 * 1024 * 1024

HEAD_W = LANES
HALF_ROPE = QK_ROPE // 2
Q_SCALE = math.log2(math.e) / math.sqrt(QK_NOPE + QK_ROPE)

BF16 = jnp.bfloat16
F32 = jnp.float32


def _rms(x, g):
    ms = jnp.mean(x * x, axis=-1, keepdims=True)
    return x * lax.rsqrt(ms + EPS) * g


def _sigmoid(x):
    return 0.5 * (jnp.tanh(0.5 * x) + 1.0)


def _dot(a, b):
    return jnp.dot(a, b, preferred_element_type=F32)


def _dot_t(a, b):
    return lax.dot_general(a, b, (((1,), (1,)), ((), ())),
                           preferred_element_type=F32)


def _rope(g, c, s_lo, s_hi):
    up = pltpu.roll(g, LANES - HALF_ROPE, 1)
    dn = pltpu.roll(g, HALF_ROPE, 1)
    return g * c + up * s_lo + dn * s_hi


def _inproj_kernel(x_ref, tab_ref, tabt_ref, gpre_ref, wsm_ref, wbig_ref, gq_ref,
                   wqt_ref, gkv_ref, wk_ref, wvt_ref,
                   qt_ref, k_ref, vt_ref, u_ref, sga_ref, sgb_ref):
    d = x_ref.shape[1]
    xb = _rms(x_ref[...], gpre_ref[...]).astype(BF16)

    small = _dot(xb, wsm_ref[...])
    qn = _rms(small[:, :Q_RANK], gq_ref[...]).astype(BF16)
    kvn = _rms(small[:, Q_RANK:Q_RANK + KV_RANK], gkv_ref[...]).astype(BF16)
    kr = _rope(small[:, Q_RANK + KV_RANK:], tab_ref[0], tab_ref[1], tab_ref[2])

    kin = jnp.concatenate([kvn, kr.astype(BF16)], axis=1)
    k_ref[...] = _dot(kin, wk_ref[...]).astype(BF16)
    vt_ref[...] = _dot_t(wvt_ref[...], kvn).astype(BF16)

    qt = _dot_t(wqt_ref[...], qn)
    cos_t, sin_t = tabt_ref[0], tabt_ref[1]
    scale = Q_SCALE
    for h in range(N_HEADS):
        r0 = h * HEAD_W
        lo = qt[r0 + QK_NOPE:r0 + QK_NOPE + HALF_ROPE]
        hi = qt[r0 + QK_NOPE + HALF_ROPE:r0 + QK_NOPE + QK_ROPE]
        qt_ref[r0:r0 + QK_NOPE, :] = (qt[r0:r0 + QK_NOPE] * scale).astype(BF16)
        qt_ref[r0 + QK_NOPE:r0 + QK_NOPE + QK_ROPE, :] = jnp.concatenate(
            [lo * cos_t - hi * sin_t, hi * cos_t + lo * sin_t], axis=0).astype(BF16)
        qt_ref[r0 + QK_NOPE + QK_ROPE:r0 + HEAD_W, :] = jnp.zeros(
            (HEAD_W - QK_NOPE - QK_ROPE, qt.shape[1]), BF16)

    a = _dot(xb, wbig_ref[:, 0:d])
    g = _dot(xb, wbig_ref[:, d:2 * d])
    u_ref[...] = (a * _sigmoid(g)).astype(BF16)
    sga_ref[...] = _sigmoid(_dot(xb, wbig_ref[:, 2 * d:3 * d])).astype(BF16)
    sgb_ref[...] = _sigmoid(_dot(xb, wbig_ref[:, 3 * d:4 * d])).astype(BF16)


def _const_spec(shape):
    nd = len(shape)
    return pl.BlockSpec(shape, lambda *_: (0,) * nd, pipeline_mode=pl.Buffered(1))


def _inproj(x2d, tabs, tabs_t, weights, tm, seq):
    rows, d = x2d.shape
    gpre, wsm, wbig, gq, wqt, gkv, wk, wvt = weights
    n_q = wqt.shape[0]
    n_v = wvt.shape[0]
    tps = seq // tm
    row_spec = lambda w: pl.BlockSpec((tm, w), lambda i: (i, 0))
    col_spec = lambda h: pl.BlockSpec((None, h, tm), lambda i: (i // tps, 0, i % tps))
    out_shapes = [
        jax.ShapeDtypeStruct((rows // seq, n_q, seq), BF16),
        jax.ShapeDtypeStruct((rows, n_q), BF16),
        jax.ShapeDtypeStruct((rows // seq, n_v, seq), BF16),
        jax.ShapeDtypeStruct((rows, d), BF16),
        jax.ShapeDtypeStruct((rows, d), BF16),
        jax.ShapeDtypeStruct((rows, d), BF16),
    ]
    return pl.pallas_call(
        _inproj_kernel,
        grid=(rows // tm,),
        in_specs=[
            row_spec(d),
            pl.BlockSpec((3, tm, LANES), lambda i: (0, i % tps, 0)),
            pl.BlockSpec((2, HALF_ROPE, tm), lambda i: (0, 0, i % tps)),
            _const_spec(gpre.shape), _const_spec(wsm.shape),
            _const_spec(wbig.shape), _const_spec(gq.shape),
            _const_spec(wqt.shape), _const_spec(gkv.shape),
            _const_spec(wk.shape), _const_spec(wvt.shape),
        ],
        out_specs=[col_spec(n_q), row_spec(n_q), col_spec(n_v),
                   row_spec(d), row_spec(d), row_spec(d)],
        out_shape=out_shapes,
        compiler_params=pltpu.CompilerParams(
            dimension_semantics=("parallel",), vmem_limit_bytes=VMEM_LIMIT),
        name="inproj",
    )(x2d, tabs, tabs_t, gpre, wsm, wbig, gq, wqt, gkv, wk, wvt)


ATTN_LOOKAHEAD = 3


def _attn_kernel(qt_ref, k_ref, vt_ref, km_ref, vmt_ref, o_ref, *, tq):
    s_len = k_ref.shape[0]
    key_chunk = lax.broadcasted_iota(jnp.int32, (tq, tq), 0) // CHUNK
    qry_chunk = lax.broadcasted_iota(jnp.int32, (tq, tq), 1) // CHUNK
    diag_ok = key_chunk <= qry_chunk

    def scores(i, h):
        q0 = i * tq
        ksl = slice(h * HEAD_W, (h + 1) * HEAD_W)
        qt = qt_ref[ksl, q0:q0 + tq]
        s_m = _dot(km_ref[:, ksl], qt)
        s_d = jnp.where(diag_ok, _dot(k_ref[q0:q0 + tq, ksl], qt), NEG_INF)
        m = jnp.maximum(jnp.max(s_m, axis=0, keepdims=True),
                        jnp.max(s_d, axis=0, keepdims=True))
        s_o = None
        if i > 0:
            s_o = _dot(k_ref[0:q0, ksl], qt)
            m = jnp.maximum(m, jnp.max(s_o, axis=0, keepdims=True))
        return s_m, s_d, s_o, m

    def weighted_values(i, h, s_m, s_d, s_o, m):
        q0 = i * tq
        vsl = slice(h * V_DIM, (h + 1) * V_DIM)
        p_m = jnp.exp2(s_m - m)
        p_d = jnp.exp2(s_d - m)
        l = jnp.sum(p_m, axis=0, keepdims=True) + jnp.sum(p_d, axis=0, keepdims=True)
        acc = (_dot(vmt_ref[vsl, :], p_m.astype(BF16))
               + _dot(vt_ref[vsl, q0:q0 + tq], p_d.astype(BF16)))
        if s_o is not None:
            p_o = jnp.exp2(s_o - m)
            l = l + jnp.sum(p_o, axis=0, keepdims=True)
            acc = acc + _dot(vt_ref[vsl, 0:q0], p_o.astype(BF16))
        return acc * (1.0 / l)

    order = [(i, h) for i in range(s_len // tq) for h in (0, 1)]
    outs = {}
    pending = [scores(*order[n]) for n in range(min(ATTN_LOOKAHEAD, len(order)))]
    for n, (i, h) in enumerate(order):
        if n + ATTN_LOOKAHEAD < len(order):
            pending.append(scores(*order[n + ATTN_LOOKAHEAD]))
        outs[h] = weighted_values(i, h, *pending.pop(0))
        if h == 1:
            ot = jnp.concatenate([outs[0], outs[1]], axis=0)
            o_ref[i * tq:(i + 1) * tq, :] = ot.T.astype(o_ref.dtype)


def _attention(qt, k, vt, km, vmt, tq):
    b, s, _ = k.shape
    n_pairs = N_HEADS // 2
    return pl.pallas_call(
        functools.partial(_attn_kernel, tq=tq),
        grid=(b, n_pairs),
        in_specs=[
            pl.BlockSpec((None, 2 * HEAD_W, s), lambda bi, hp: (bi, hp, 0)),
            pl.BlockSpec((None, s, 2 * HEAD_W), lambda bi, hp: (bi, 0, hp)),
            pl.BlockSpec((None, 2 * V_DIM, s), lambda bi, hp: (bi, hp, 0)),
            pl.BlockSpec((N_META, 2 * HEAD_W), lambda bi, hp: (0, hp)),
            pl.BlockSpec((2 * V_DIM, N_META), lambda bi, hp: (hp, 0)),
        ],
        out_specs=pl.BlockSpec((None, s, 2 * V_DIM), lambda bi, hp: (bi, 0, hp)),
        out_shape=jax.ShapeDtypeStruct((b, s, N_HEADS * V_DIM), BF16),
        compiler_params=pltpu.CompilerParams(
            dimension_semantics=("parallel", "parallel"),
            vmem_limit_bytes=VMEM_LIMIT),
        name="attention",
    )(qt, k, vt, km, vmt)


CONV_PAD = 32
CONV_RB = 64


def _conv_kernel(u_ref, um_ref, w_ref, b_ref, y_ref, buf_ref):
    s = u_ref.shape[0]
    buf_ref[0:CONV_PAD - N_META, :] = jnp.zeros((CONV_PAD - N_META, LANES), F32)
    buf_ref[CONV_PAD - N_META:CONV_PAD, :] = um_ref[...].astype(F32)
    buf_ref[CONV_PAD:, :] = u_ref[...].astype(F32)
    base = CONV_PAD - (CONV_K - 1)
    bias = b_ref[...]

    def block(rb, _):
        r0 = pl.multiple_of(rb * CONV_RB, CONV_RB)
        acc = jnp.broadcast_to(bias, (CONV_RB, LANES))
        for kk in range(CONV_K):
            acc = acc + w_ref[kk:kk + 1, :] * buf_ref[pl.ds(r0 + base + kk, CONV_RB), :]
        y_ref[pl.ds(r0, CONV_RB), :] = acc.astype(y_ref.dtype)
        return 0

    lax.fori_loop(0, s // CONV_RB, block, 0)


def _conv(u, um, conv_w, conv_b):
    b, s, c = u.shape
    return pl.pallas_call(
        _conv_kernel,
        grid=(b, c // LANES),
        in_specs=[
            pl.BlockSpec((None, s, LANES), lambda bi, ci: (bi, 0, ci)),
            pl.BlockSpec((N_META, LANES), lambda bi, ci: (0, ci)),
            pl.BlockSpec((CONV_K, LANES), lambda bi, ci: (0, ci)),
            pl.BlockSpec((1, LANES), lambda bi, ci: (0, ci)),
        ],
        out_specs=pl.BlockSpec((None, s, LANES), lambda bi, ci: (bi, 0, ci)),
        out_shape=jax.ShapeDtypeStruct((b, s, c), BF16),
        scratch_shapes=[pltpu.VMEM((s + CONV_PAD, LANES), F32)],
        compiler_params=pltpu.CompilerParams(
            dimension_semantics=("parallel", "parallel"),
            vmem_limit_bytes=VMEM_LIMIT),
        name="dwconv",
    )(u, um, conv_w, conv_b)


FF_CHUNK = 256


def _tail_kernel(x_ref, o_ref, yc_ref, sga_ref, sgb_ref,
                 lng_ref, lnb_ref, wao_ref, wco_ref, wout_ref, gpost_ref,
                 gfpre_ref, wfin_ref, wfout_ref, gfpost_ref,
                 out_ref, act_ref):
    d_ff = wfout_ref.shape[0]

    yc = yc_ref[...].astype(F32)
    mu = jnp.mean(yc, axis=-1, keepdims=True)
    xc = yc - mu
    var = jnp.mean(xc * xc, axis=-1, keepdims=True)
    z = xc * lax.rsqrt(var + EPS) * lng_ref[...] + lnb_ref[...]
    z = (z * _sigmoid(z)).astype(BF16)

    y_a = _dot(o_ref[...], wao_ref[...])
    y_b = _dot(z, wco_ref[...])
    merged = sga_ref[...].astype(F32) * y_a + sgb_ref[...].astype(F32) * y_b
    mo = _dot(merged.astype(BF16), wout_ref[...])
    x1 = x_ref[...] + _rms(mo, gpost_ref[...])

    hb = _rms(x1, gfpre_ref[...]).astype(BF16)
    for c0 in range(0, d_ff, FF_CHUNK):
        g = _dot(hb, wfin_ref[:, c0:c0 + FF_CHUNK])
        up = _dot(hb, wfin_ref[:, d_ff + c0:d_ff + c0 + FF_CHUNK])
        act_ref[:, c0:c0 + FF_CHUNK] = (g * _sigmoid(g) * up).astype(BF16)
    f = _dot(act_ref[...], wfout_ref[...])
    out_ref[...] = x1 + _rms(f, gfpost_ref[...])


def _tail(x2d, o2d, yc2d, sga, sgb, weights, tm):
    rows, d = x2d.shape
    d_ff = weights[8].shape[0]
    row_spec = pl.BlockSpec((tm, d), lambda i: (i, 0))
    return pl.pallas_call(
        _tail_kernel,
        grid=(rows // tm,),
        in_specs=[row_spec] * 5 + [_const_spec(w.shape) for w in weights],
        out_specs=row_spec,
        out_shape=jax.ShapeDtypeStruct((rows, d), F32),
        scratch_shapes=[pltpu.VMEM((tm, d_ff), BF16)],
        compiler_params=pltpu.CompilerParams(
            dimension_semantics=("parallel",), vmem_limit_bytes=VMEM_LIMIT),
        name="tail",
    )(x2d, o2d, yc2d, sga, sgb, *weights)


def _rope_tables(length):
    pos = jnp.arange(length, dtype=F32)
    inv = ROPE_BASE ** (-jnp.arange(0, QK_ROPE, 2, dtype=F32) / QK_ROPE)
    ang = pos[:, None] * inv[None, :]
    cos, sin = jnp.cos(ang), jnp.sin(ang)
    zeros = lambda w: jnp.zeros((length, w), F32)
    ktail = LANES - QK_ROPE
    ck = jnp.concatenate([cos, cos, zeros(ktail)], axis=1)
    sk_lo = jnp.concatenate([-sin, zeros(HALF_ROPE + ktail)], axis=1)
    sk_hi = jnp.concatenate([zeros(HALF_ROPE), sin, zeros(ktail)], axis=1)
    return jnp.stack([ck, sk_lo, sk_hi]), jnp.stack([cos.T, sin.T]) * Q_SCALE


def kernel(x, meta, mix_pre_g, w_in, q_norm_g, w_uq, kv_norm_g, w_ukv, w_attn_o,
           conv_w, conv_b, conv_ln_g, conv_ln_b, w_conv_o, w_out, mix_post_g,
           ffn_pre_g, w_ffn_in, w_ffn_out, ffn_post_g):
    assert w_in.shape[0] == 1, "single-layer block"
    b, s, d = x.shape
    row = lambda g: g[0][None, :].astype(F32)

    w_in0 = w_in[0]
    off_glu = Q_RANK + KV_RANK + QK_ROPE
    wsm = jnp.pad(w_in0[:, :off_glu], ((0, 0), (0, 4 * LANES - off_glu))).astype(BF16)
    wbig = w_in0[:, off_glu:].astype(BF16)

    qk_dim = QK_NOPE + QK_ROPE
    wq = w_uq[0].reshape(Q_RANK, N_HEADS, qk_dim)
    wq = jnp.pad(wq, ((0, 0), (0, 0), (0, HEAD_W - qk_dim)))
    wqt = wq.reshape(Q_RANK, N_HEADS * HEAD_W).T.astype(BF16)

    wkv = w_ukv[0].reshape(KV_RANK, N_HEADS, QK_NOPE + V_DIM)
    wk_nope = jnp.pad(wkv[:, :, :QK_NOPE], ((0, 0), (0, 0), (0, HEAD_W - QK_NOPE)))
    place = jnp.zeros((LANES, N_HEADS, HEAD_W), F32)
    jj = jnp.arange(QK_ROPE)
    place = place.at[jj, :, QK_NOPE + jj].set(1.0)
    wk = jnp.concatenate([wk_nope, place], axis=0)
    wk = wk.reshape(KV_RANK + LANES, N_HEADS * HEAD_W).astype(BF16)
    wvt = wkv[:, :, QK_NOPE:].reshape(KV_RANK, N_HEADS * V_DIM).T.astype(BF16)

    in_weights = (row(mix_pre_g), wsm, wbig, row(q_norm_g), wqt, row(kv_norm_g), wk, wvt)

    tabs, tabs_t = _rope_tables(N_META + s)

    tm = 256
    x2d = x.reshape(b * s, d)
    qt, k, vt, u, sga, sgb = _inproj(x2d, tabs[:, N_META:], tabs_t[:, :, N_META:],
                                     in_weights, tm, s)
    _, km, vmt, um, _, _ = _inproj(meta.astype(F32), tabs[:, :N_META],
                                   tabs_t[:, :, :N_META], in_weights, N_META, N_META)

    o = _attention(qt, k.reshape(b, s, -1), vt, km, vmt[0], tq=256)

    yc = _conv(u.reshape(b, s, d), um, conv_w[0].astype(F32), row(conv_b))

    tail_weights = (row(conv_ln_g), row(conv_ln_b), w_attn_o[0].astype(BF16),
                    w_conv_o[0].astype(BF16), w_out[0].astype(BF16), row(mix_post_g),
                    row(ffn_pre_g), w_ffn_in[0].astype(BF16), w_ffn_out[0].astype(BF16),
                    row(ffn_post_g))
    out = _tail(x2d, o.reshape(b * s, -1), yc.reshape(b * s, d), sga, sgb,
                tail_weights, tm)
    return out.reshape(b, s, d)
```

```python
import functools
import math

import jax
import jax.numpy as jnp
from jax import lax
from jax.experimental import pallas as pl
from jax.experimental.pallas import tpu as pltpu

CHUNK = 64
N_META = 16
N_HEADS = 16
QK_NOPE = 64
QK_ROPE = 32
V_DIM = 64
Q_RANK = 256
KV_RANK = 128
ROPE_BASE = 10000.0
CONV_K = 31
EPS = 1e-6
NEG_INF = -1e30

LANES = 128
VMEM_LIMIT = 56 * 1024 * 1024

HEAD_W = LANES
HALF_ROPE = QK_ROPE // 2
V_ROWS = V_DIM + 16
Q_SCALE = math.log2(math.e) / math.sqrt(QK_NOPE + QK_ROPE)

BF16 = jnp.bfloat16
F32 = jnp.float32


def _rms(x, g):
    ms = jnp.mean(x * x, axis=-1, keepdims=True)
    return x * lax.rsqrt(ms + EPS) * g


def _sigmoid(x):
    return 0.5 * (jnp.tanh(0.5 * x) + 1.0)


def _dot(a, b):
    return jnp.dot(a, b, preferred_element_type=F32)


def _dot_t(a, b):
    return lax.dot_general(a, b, (((1,), (1,)), ((), ())),
                           preferred_element_type=F32)


def _rope(g, c, s_lo, s_hi):
    up = pltpu.roll(g, LANES - HALF_ROPE, 1)
    dn = pltpu.roll(g, HALF_ROPE, 1)
    return g * c + up * s_lo + dn * s_hi


def _inproj_kernel(x_ref, tab_ref, tabt_ref, gpre_ref, wsm_ref, wbig_ref, gq_ref,
                   wqt_ref, gkv_ref, wk_ref, wvt_ref,
                   qt_ref, k_ref, vt_ref, u_ref, sga_ref, sgb_ref):
    d = x_ref.shape[1]
    xb = _rms(x_ref[...], gpre_ref[...]).astype(BF16)

    small = _dot(xb, wsm_ref[...])
    qn = _rms(small[:, :Q_RANK], gq_ref[...]).astype(BF16)
    kvn = _rms(small[:, Q_RANK:Q_RANK + KV_RANK], gkv_ref[...]).astype(BF16)
    kr = _rope(small[:, Q_RANK + KV_RANK:], tab_ref[0], tab_ref[1], tab_ref[2])

    kin = jnp.concatenate([kvn, kr.astype(BF16)], axis=1)
    k_ref[...] = _dot(kin, wk_ref[...]).astype(BF16)
    vt = _dot_t(wvt_ref[...], kvn).astype(BF16)
    for h in range(N_HEADS):
        vt_ref[h * V_ROWS:h * V_ROWS + V_DIM, :] = vt[h * V_DIM:(h + 1) * V_DIM]
        vt_ref[h * V_ROWS + V_DIM:(h + 1) * V_ROWS, :] = jnp.ones(
            (V_ROWS - V_DIM, vt.shape[1]), BF16)

    qt = _dot_t(wqt_ref[...], qn)
    cos_t, sin_t = tabt_ref[0], tabt_ref[1]
    scale = Q_SCALE
    for h in range(N_HEADS):
        r0 = h * HEAD_W
        lo = qt[r0 + QK_NOPE:r0 + QK_NOPE + HALF_ROPE]
        hi = qt[r0 + QK_NOPE + HALF_ROPE:r0 + QK_NOPE + QK_ROPE]
        qt_ref[r0:r0 + QK_NOPE, :] = (qt[r0:r0 + QK_NOPE] * scale).astype(BF16)
        qt_ref[r0 + QK_NOPE:r0 + QK_NOPE + QK_ROPE, :] = jnp.concatenate(
            [lo * cos_t - hi * sin_t, hi * cos_t + lo * sin_t], axis=0).astype(BF16)
        qt_ref[r0 + QK_NOPE + QK_ROPE:r0 + HEAD_W, :] = jnp.zeros(
            (HEAD_W - QK_NOPE - QK_ROPE, qt.shape[1]), BF16)

    a = _dot(xb, wbig_ref[:, 0:d])
    g = _dot(xb, wbig_ref[:, d:2 * d])
    u_ref[...] = (a * _sigmoid(g)).astype(BF16)
    sga_ref[...] = _sigmoid(_dot(xb, wbig_ref[:, 2 * d:3 * d])).astype(BF16)
    sgb_ref[...] = _sigmoid(_dot(xb, wbig_ref[:, 3 * d:4 * d])).astype(BF16)


def _const_spec(shape):
    nd = len(shape)
    return pl.BlockSpec(shape, lambda *_: (0,) * nd, pipeline_mode=pl.Buffered(1))


def _inproj(x2d, tabs, tabs_t, weights, tm, seq):
    rows, d = x2d.shape
    gpre, wsm, wbig, gq, wqt, gkv, wk, wvt = weights
    n_q = wqt.shape[0]
    n_v = N_HEADS * V_ROWS
    tps = seq // tm
    row_spec = lambda w: pl.BlockSpec((tm, w), lambda i: (i, 0))
    col_spec = lambda h: pl.BlockSpec((None, h, tm), lambda i: (i // tps, 0, i % tps))
    out_shapes = [
        jax.ShapeDtypeStruct((rows // seq, n_q, seq), BF16),
        jax.ShapeDtypeStruct((rows, n_q), BF16),
        jax.ShapeDtypeStruct((rows // seq, n_v, seq), BF16),
        jax.ShapeDtypeStruct((rows, d), BF16),
        jax.ShapeDtypeStruct((rows, d), BF16),
        jax.ShapeDtypeStruct((rows, d), BF16),
    ]
    return pl.pallas_call(
        _inproj_kernel,
        grid=(rows // tm,),
        in_specs=[
            row_spec(d),
            pl.BlockSpec((3, tm, LANES), lambda i: (0, i % tps, 0)),
            pl.BlockSpec((2, HALF_ROPE, tm), lambda i: (0, 0, i % tps)),
            _const_spec(gpre.shape), _const_spec(wsm.shape),
            _const_spec(wbig.shape), _const_spec(gq.shape),
            _const_spec(wqt.shape), _const_spec(gkv.shape),
            _const_spec(wk.shape), _const_spec(wvt.shape),
        ],
        out_specs=[col_spec(n_q), row_spec(n_q), col_spec(n_v),
                   row_spec(d), row_spec(d), row_spec(d)],
        out_shape=out_shapes,
        compiler_params=pltpu.CompilerParams(
            dimension_semantics=("parallel",), vmem_limit_bytes=VMEM_LIMIT),
        name="inproj",
    )(x2d, tabs, tabs_t, gpre, wsm, wbig, gq, wqt, gkv, wk, wvt)


ATTN_LOOKAHEAD = 3


def _attn_kernel(qt_ref, k_ref, vt_ref, km_ref, vmt_ref, o_ref, *, tq):
    s_len = k_ref.shape[0]
    key_chunk = lax.broadcasted_iota(jnp.int32, (tq, tq), 0) // CHUNK
    qry_chunk = lax.broadcasted_iota(jnp.int32, (tq, tq), 1) // CHUNK
    diag_ok = key_chunk <= qry_chunk

    def scores(i, h):
        q0 = i * tq
        ksl = slice(h * HEAD_W, (h + 1) * HEAD_W)
        qt = qt_ref[ksl, q0:q0 + tq]
        s_m = _dot(km_ref[:, ksl], qt)
        s_d = jnp.where(diag_ok, _dot(k_ref[q0:q0 + tq, ksl], qt), NEG_INF)
        m = jnp.maximum(jnp.max(s_m, axis=0, keepdims=True),
                        jnp.max(s_d, axis=0, keepdims=True))
        s_o = None
        if i > 0:
            s_o = _dot(k_ref[0:q0, ksl], qt)
            m = jnp.maximum(m, jnp.max(s_o, axis=0, keepdims=True))
        return s_m, s_d, s_o, m

    def weighted_values(i, h, s_m, s_d, s_o, m):
        q0 = i * tq
        vsl = slice(h * V_ROWS, (h + 1) * V_ROWS)
        p_m = jnp.exp2(s_m - m)
        p_d = jnp.exp2(s_d - m)
        acc = (_dot(vmt_ref[vsl, :], p_m.astype(BF16))
               + _dot(vt_ref[vsl, q0:q0 + tq], p_d.astype(BF16)))
        if s_o is not None:
            p_o = jnp.exp2(s_o - m)
            acc = acc + _dot(vt_ref[vsl, 0:q0], p_o.astype(BF16))
        return acc[:V_DIM] * (1.0 / acc[V_DIM:V_DIM + 1])

    order = [(i, h) for i in reversed(range(s_len // tq)) for h in (0, 1)]
    outs = {}
    pending = [scores(*order[n]) for n in range(min(ATTN_LOOKAHEAD, len(order)))]
    for n, (i, h) in enumerate(order):
        if n + ATTN_LOOKAHEAD < len(order):
            pending.append(scores(*order[n + ATTN_LOOKAHEAD]))
        outs[h] = weighted_values(i, h, *pending.pop(0))
        if h == 1:
            ot = jnp.concatenate([outs[0], outs[1]], axis=0)
            o_ref[i * tq:(i + 1) * tq, :] = ot.T.astype(o_ref.dtype)


def _attention(qt, k, vt, km, vmt, tq):
    b, s, _ = k.shape
    n_pairs = N_HEADS // 2
    return pl.pallas_call(
        functools.partial(_attn_kernel, tq=tq),
        grid=(b, n_pairs),
        in_specs=[
            pl.BlockSpec((None, 2 * HEAD_W, s), lambda bi, hp: (bi, hp, 0)),
            pl.BlockSpec((None, s, 2 * HEAD_W), lambda bi, hp: (bi, 0, hp)),
            pl.BlockSpec((None, 2 * V_ROWS, s), lambda bi, hp: (bi, hp, 0)),
            pl.BlockSpec((N_META, 2 * HEAD_W), lambda bi, hp: (0, hp)),
            pl.BlockSpec((2 * V_ROWS, N_META), lambda bi, hp: (hp, 0)),
        ],
        out_specs=pl.BlockSpec((None, s, 2 * V_DIM), lambda bi, hp: (bi, 0, hp)),
        out_shape=jax.ShapeDtypeStruct((b, s, N_HEADS * V_DIM), BF16),
        compiler_params=pltpu.CompilerParams(
            dimension_semantics=("parallel", "parallel"),
            vmem_limit_bytes=VMEM_LIMIT),
        name="attention",
    )(qt, k, vt, km, vmt)


CONV_PAD = 32
CONV_RB = 64
FF_CHUNK = 256
FF_GROUP = 4


def _tail_kernel(x_ref, o_ref, sga_ref, sgb_ref, u_ref, halo_ref, mhalo_ref,
                 cw_ref, cb_ref, lng_ref, lnb_ref, wao_ref, wco_ref, wout_ref,
                 gpost_ref, gfpre_ref, wfin_ref, wfout_ref, gfpost_ref,
                 out_ref, buf_ref, mix_ref, act_ref, *, tiles_per_seq, n_tiles):
    s = pl.program_id(0)
    tm, d = u_ref.shape
    d_ff = wfout_ref.shape[0]

    @pl.when(s == 0)
    def _():
        mix_ref[...] = jnp.zeros(mix_ref.shape, mix_ref.dtype)

    conv_tile = jnp.minimum(s, n_tiles - 1)
    first_in_seq = conv_tile % tiles_per_seq == 0
    row0 = jnp.minimum(s, 0)

    hist = jnp.where(first_in_seq, mhalo_ref[...], halo_ref[...]).astype(F32)
    for cb in range(d // LANES):
        csl = slice(cb * LANES, (cb + 1) * LANES)
        buf_ref[cb, 0:CONV_PAD, :] = hist[:, csl]
        buf_ref[cb, CONV_PAD:CONV_PAD + tm, :] = u_ref[:, csl].astype(F32)

    base = CONV_PAD - (CONV_K - 1)
    pieces = [(cb, r0) for cb in range(d // LANES) for r0 in range(0, tm, CONV_RB)]
    n_pieces = len(pieces)
    n_chunks = d_ff // FF_CHUNK
    n_groups = -(-n_chunks // FF_GROUP)
    wide, narrow = d // FF_CHUNK, 1
    budget = (2 * n_chunks - 1) * narrow + (n_groups - 1) * wide
    spent = [0]
    yc_c0, hb_c0 = 0, d
    gate_row = pl.multiple_of(row0, 16)

    def gated(c0):
        return mix_ref[pl.ds(gate_row, tm), c0:c0 + d]

    def conv_pieces(cost):
        lo = min(spent[0], budget) * n_pieces // budget
        spent[0] += cost
        hi = min(spent[0], budget) * n_pieces // budget
        for _ in range(hi - lo):
            cb, r0 = pieces.pop(0)
            csl = slice(cb * LANES, (cb + 1) * LANES)
            acc = jnp.broadcast_to(cb_ref[:, csl], (CONV_RB, LANES))
            for res in range(8):
                taps = list(range(res, CONV_K, 8))
                n_rows = CONV_RB + 8 * (len(taps) - 1)
                win = buf_ref[cb, pl.ds(row0 + (r0 + base + res), n_rows), :]
                for t, kk in enumerate(taps):
                    acc = acc + cw_ref[kk:kk + 1, csl] * win[8 * t:8 * t + CONV_RB]
            mix_ref[r0:r0 + CONV_RB, yc_c0 + cb * LANES:yc_c0 + (cb + 1) * LANES] = acc.astype(BF16)

    yc = mix_ref[:, yc_c0:yc_c0 + d].astype(F32)
    mu = jnp.mean(yc, axis=-1, keepdims=True)
    xc = yc - mu
    var = jnp.mean(xc * xc, axis=-1, keepdims=True)
    z = xc * lax.rsqrt(var + EPS) * lng_ref[...] + lnb_ref[...]
    z = (z * _sigmoid(z)).astype(BF16)

    y_a = _dot(o_ref[...], wao_ref[...])
    y_b = _dot(z, wco_ref[...])
    merged = sga_ref[...].astype(F32) * y_a + sgb_ref[...].astype(F32) * y_b
    mo = _dot(merged.astype(BF16), wout_ref[...])
    x1 = x_ref[...] + _rms(mo, gpost_ref[...])

    mix_ref[:, hb_c0:hb_c0 + d] = _rms(x1, gfpre_ref[...]).astype(BF16)
    f = None
    for grp in range(n_groups):
        g0 = grp * FF_GROUP * FF_CHUNK
        g1 = min(g0 + FF_GROUP * FF_CHUNK, d_ff)
        for c0 in range(g0, g1, FF_CHUNK):
            g = _dot(gated(hb_c0), wfin_ref[:, c0:c0 + FF_CHUNK])
            conv_pieces(narrow)
            up = _dot(gated(hb_c0), wfin_ref[:, d_ff + c0:d_ff + c0 + FF_CHUNK])
            conv_pieces(narrow)
            act_ref[:, c0:c0 + FF_CHUNK] = (g * _sigmoid(g) * up).astype(BF16)
        part = _dot(act_ref[:, g0:g1], wfout_ref[g0:g1, :])
        conv_pieces(wide)
        f = part if f is None else f + part
    assert not pieces
    out_ref[...] = x1 + _rms(f, gfpost_ref[...])


def _tail(x2d, o2d, sga, sgb, u2d, mhalo, weights, tm, seq):
    rows, d = x2d.shape
    d_ff = weights[10].shape[0]
    n_tiles = rows // tm
    halo_per_tile = tm // CONV_PAD
    lag_spec = pl.BlockSpec((tm, d), lambda s: (jnp.maximum(s - 1, 0), 0))
    conv_tile = lambda s: jnp.minimum(s, n_tiles - 1)
    return pl.pallas_call(
        functools.partial(_tail_kernel, tiles_per_seq=seq // tm, n_tiles=n_tiles),
        grid=(n_tiles + 1,),
        in_specs=[lag_spec] * 4 + [
            pl.BlockSpec((tm, d), lambda s: (conv_tile(s), 0)),
            pl.BlockSpec((CONV_PAD, d),
                         lambda s: (jnp.maximum(conv_tile(s) * halo_per_tile - 1, 0), 0)),
        ] + [_const_spec(mhalo.shape)] + [_const_spec(w.shape) for w in weights],
        out_specs=lag_spec,
        out_shape=jax.ShapeDtypeStruct((rows, d), F32),
        scratch_shapes=[pltpu.VMEM((d // LANES, tm + CONV_PAD, LANES), F32),
                        pltpu.VMEM((tm, 2 * d), BF16),
                        pltpu.VMEM((tm, d_ff), BF16)],
        compiler_params=pltpu.CompilerParams(
            dimension_semantics=("arbitrary",), vmem_limit_bytes=VMEM_LIMIT),
        name="tail",
    )(x2d, o2d, sga, sgb, u2d, u2d, mhalo, *weights)


def _rope_tables(length):
    pos = jnp.arange(length, dtype=F32)
    inv = ROPE_BASE ** (-jnp.arange(0, QK_ROPE, 2, dtype=F32) / QK_ROPE)
    ang = pos[:, None] * inv[None, :]
    cos, sin = jnp.cos(ang), jnp.sin(ang)
    zeros = lambda w: jnp.zeros((length, w), F32)
    ktail = LANES - QK_ROPE
    ck = jnp.concatenate([cos, cos, zeros(ktail)], axis=1)
    sk_lo = jnp.concatenate([-sin, zeros(HALF_ROPE + ktail)], axis=1)
    sk_hi = jnp.concatenate([zeros(HALF_ROPE), sin, zeros(ktail)], axis=1)
    return jnp.stack([ck, sk_lo, sk_hi]), jnp.stack([cos.T, sin.T]) * Q_SCALE


def kernel(x, meta, mix_pre_g, w_in, q_norm_g, w_uq, kv_norm_g, w_ukv, w_attn_o,
           conv_w, conv_b, conv_ln_g, conv_ln_b, w_conv_o, w_out, mix_post_g,
           ffn_pre_g, w_ffn_in, w_ffn_out, ffn_post_g):
    assert w_in.shape[0] == 1, "single-layer block"
    b, s, d = x.shape
    row = lambda g: g[0][None, :].astype(F32)

    w_in0 = w_in[0]
    off_glu = Q_RANK + KV_RANK + QK_ROPE
    wsm = jnp.pad(w_in0[:, :off_glu], ((0, 0), (0, 4 * LANES - off_glu))).astype(BF16)
    wbig = w_in0[:, off_glu:].astype(BF16)

    qk_dim = QK_NOPE + QK_ROPE
    wq = w_uq[0].reshape(Q_RANK, N_HEADS, qk_dim)
    wq = jnp.pad(wq, ((0, 0), (0, 0), (0, HEAD_W - qk_dim)))
    wqt = wq.reshape(Q_RANK, N_HEADS * HEAD_W).T.astype(BF16)

    wkv = w_ukv[0].reshape(KV_RANK, N_HEADS, QK_NOPE + V_DIM)
    wk_nope = jnp.pad(wkv[:, :, :QK_NOPE], ((0, 0), (0, 0), (0, HEAD_W - QK_NOPE)))
    place = jnp.zeros((LANES, N_HEADS, HEAD_W), F32)
    jj = jnp.arange(QK_ROPE)
    place = place.at[jj, :, QK_NOPE + jj].set(1.0)
    wk = jnp.concatenate([wk_nope, place], axis=0)
    wk = wk.reshape(KV_RANK + LANES, N_HEADS * HEAD_W).astype(BF16)
    wvt = wkv[:, :, QK_NOPE:].reshape(KV_RANK, N_HEADS * V_DIM).T.astype(BF16)

    in_weights = (row(mix_pre_g), wsm, wbig, row(q_norm_g), wqt, row(kv_norm_g), wk, wvt)

    tabs, tabs_t = _rope_tables(N_META + s)

    tm = 256
    x2d = x.reshape(b * s, d)
    qt, k, vt, u, sga, sgb = _inproj(x2d, tabs[:, N_META:], tabs_t[:, :, N_META:],
                                     in_weights, tm, s)
    _, km, vmt, um, _, _ = _inproj(meta.astype(F32), tabs[:, :N_META],
                                   tabs_t[:, :, :N_META], in_weights, N_META, N_META)

    o = _attention(qt, k.reshape(b, s, -1), vt, km, vmt[0], tq=256)

    mhalo = jnp.concatenate([jnp.zeros((CONV_PAD - N_META, d), BF16), um], axis=0)
    tail_weights = (conv_w[0].astype(F32), row(conv_b), row(conv_ln_g), row(conv_ln_b),
                    w_attn_o[0].astype(BF16), w_conv_o[0].astype(BF16),
                    w_out[0].astype(BF16), row(mix_post_g), row(ffn_pre_g),
                    w_ffn_in[0].astype(BF16), w_ffn_out[0].astype(BF16), row(ffn_post_g))
    out = _tail(x2d, o.reshape(b * s, -1), sga, sgb, u, mhalo, tail_weights, tm, s)
    return out.reshape(b, s, d)
```

```python
import functools
import math

import jax
import jax.numpy as jnp
import numpy as np
from jax import lax
from jax.experimental import pallas as pl
from jax.experimental.pallas import tpu as pltpu

CHUNK = 64
N_META = 16
N_HEADS = 16
QK_NOPE = 64
QK_ROPE = 32
V_DIM = 64
Q_RANK = 256
KV_RANK = 128
ROPE_BASE = 10000.0
CONV_K = 31
EPS = 1e-6
NEG_INF = -1e30

LANES = 128
VMEM_LIMIT = 56 * 1024 * 1024

HEAD_W = LANES
HALF_ROPE = QK_ROPE // 2
V_ROWS = V_DIM + 16
Q_SCALE = math.log2(math.e) / math.sqrt(QK_NOPE + QK_ROPE)

BF16 = jnp.bfloat16
F32 = jnp.float32


def _rms(x, g):
    ms = jnp.mean(x * x, axis=-1, keepdims=True)
    return x * lax.rsqrt(ms + EPS) * g


def _sigmoid(x):
    return 0.5 * (jnp.tanh(0.5 * x) + 1.0)


def _dot(a, b):
    return jnp.dot(a, b, preferred_element_type=F32)


def _dot_t(a, b):
    return lax.dot_general(a, b, (((1,), (1,)), ((), ())),
                           preferred_element_type=F32)


def _rope(g, c, s_lo, s_hi):
    up = pltpu.roll(g, LANES - HALF_ROPE, 1)
    dn = pltpu.roll(g, HALF_ROPE, 1)
    return g * c + up * s_lo + dn * s_hi


def _inproj_kernel(x_ref, tab_ref, tabt_ref, gpre_ref, wsm_ref, wbig_ref, gq_ref,
                   wqt_ref, gkv_ref, wk_ref, wvt_ref,
                   qt_ref, k_ref, vt_ref, u_ref, sga_ref, sgb_ref):
    d = x_ref.shape[1]
    xb = _rms(x_ref[...], gpre_ref[...]).astype(BF16)

    small = _dot(xb, wsm_ref[...])
    a = _dot(xb, wbig_ref[:, 0:d])
    g = _dot(xb, wbig_ref[:, d:2 * d])
    u_ref[...] = (a * _sigmoid(g)).astype(BF16)
    qn = _rms(small[:, :Q_RANK], gq_ref[...]).astype(BF16)
    kvn = _rms(small[:, Q_RANK:Q_RANK + KV_RANK], gkv_ref[...]).astype(BF16)
    kr = _rope(small[:, Q_RANK + KV_RANK:], tab_ref[0], tab_ref[1], tab_ref[2])

    kin = jnp.concatenate([kvn, kr.astype(BF16)], axis=1)
    k_ref[...] = _dot(kin, wk_ref[...]).astype(BF16)
    vt = _dot_t(wvt_ref[...], kvn).astype(BF16)
    for h in range(N_HEADS):
        vt_ref[h * V_ROWS:h * V_ROWS + V_DIM, :] = vt[h * V_DIM:(h + 1) * V_DIM]
        vt_ref[h * V_ROWS + V_DIM:(h + 1) * V_ROWS, :] = jnp.ones(
            (V_ROWS - V_DIM, vt.shape[1]), BF16)

    qt = _dot_t(wqt_ref[...], qn)
    cos_t, sin_t = tabt_ref[0], tabt_ref[1]
    scale = Q_SCALE
    for h in range(N_HEADS):
        r0 = h * HEAD_W
        lo = qt[r0 + QK_NOPE:r0 + QK_NOPE + HALF_ROPE]
        hi = qt[r0 + QK_NOPE + HALF_ROPE:r0 + QK_NOPE + QK_ROPE]
        qt_ref[r0:r0 + QK_NOPE, :] = (qt[r0:r0 + QK_NOPE] * scale).astype(BF16)
        qt_ref[r0 + QK_NOPE:r0 + QK_NOPE + QK_ROPE, :] = jnp.concatenate(
            [lo * cos_t - hi * sin_t, hi * cos_t + lo * sin_t], axis=0).astype(BF16)
        qt_ref[r0 + QK_NOPE + QK_ROPE:r0 + HEAD_W, :] = jnp.zeros(
            (HEAD_W - QK_NOPE - QK_ROPE, qt.shape[1]), BF16)

    sga_ref[...] = _sigmoid(_dot(xb, wbig_ref[:, 2 * d:3 * d])).astype(BF16)
    sgb_ref[...] = _sigmoid(_dot(xb, wbig_ref[:, 3 * d:4 * d])).astype(BF16)


def _const_spec(shape):
    nd = len(shape)
    return pl.BlockSpec(shape, lambda *_: (0,) * nd, pipeline_mode=pl.Buffered(1))


def _inproj(x2d, tabs, tabs_t, weights, tm, seq):
    rows, d = x2d.shape
    gpre, wsm, wbig, gq, wqt, gkv, wk, wvt = weights
    n_q = wqt.shape[0]
    n_v = N_HEADS * V_ROWS
    tps = seq // tm
    row_spec = lambda w: pl.BlockSpec((tm, w), lambda i: (i, 0))
    col_spec = lambda h: pl.BlockSpec((None, h, tm), lambda i: (i // tps, 0, i % tps))
    out_shapes = [
        jax.ShapeDtypeStruct((rows // seq, n_q, seq), BF16),
        jax.ShapeDtypeStruct((rows, n_q), BF16),
        jax.ShapeDtypeStruct((rows // seq, n_v, seq), BF16),
        jax.ShapeDtypeStruct((rows, d), BF16),
        jax.ShapeDtypeStruct((rows, d), BF16),
        jax.ShapeDtypeStruct((rows, d), BF16),
    ]
    return pl.pallas_call(
        _inproj_kernel,
        grid=(rows // tm,),
        in_specs=[
            row_spec(d),
            pl.BlockSpec((3, tm, LANES), lambda i: (0, i % tps, 0)),
            pl.BlockSpec((2, HALF_ROPE, tm), lambda i: (0, 0, i % tps)),
            _const_spec(gpre.shape), _const_spec(wsm.shape),
            _const_spec(wbig.shape), _const_spec(gq.shape),
            _const_spec(wqt.shape), _const_spec(gkv.shape),
            _const_spec(wk.shape), _const_spec(wvt.shape),
        ],
        out_specs=[col_spec(n_q), row_spec(n_q), col_spec(n_v),
                   row_spec(d), row_spec(d), row_spec(d)],
        out_shape=out_shapes,
        compiler_params=pltpu.CompilerParams(
            dimension_semantics=("parallel",), vmem_limit_bytes=VMEM_LIMIT),
        name="inproj",
    )(x2d, tabs, tabs_t, gpre, wsm, wbig, gq, wqt, gkv, wk, wvt)


ATTN_LOOKAHEAD = 3


def _attn_kernel(qt_ref, k_ref, vt_ref, km_ref, vmt_ref, o_ref, *, tq):
    s_len = k_ref.shape[0]
    key_chunk = lax.broadcasted_iota(jnp.int32, (tq, tq), 0) // CHUNK
    qry_chunk = lax.broadcasted_iota(jnp.int32, (tq, tq), 1) // CHUNK
    diag_ok = key_chunk <= qry_chunk

    def scores(i, h):
        q0 = i * tq
        ksl = slice(h * HEAD_W, (h + 1) * HEAD_W)
        qt = qt_ref[ksl, q0:q0 + tq]
        s_m = _dot(km_ref[:, ksl], qt)
        s_d = jnp.where(diag_ok, _dot(k_ref[q0:q0 + tq, ksl], qt), NEG_INF)
        m = jnp.maximum(jnp.max(s_m, axis=0, keepdims=True),
                        jnp.max(s_d, axis=0, keepdims=True))
        s_o = None
        if i > 0:
            s_o = _dot(k_ref[0:q0, ksl], qt)
            m = jnp.maximum(m, jnp.max(s_o, axis=0, keepdims=True))
        return s_m, s_d, s_o, m

    def weighted_values(i, h, s_m, s_d, s_o, m):
        q0 = i * tq
        vsl = slice(h * V_ROWS, (h + 1) * V_ROWS)
        p_m = jnp.exp2(s_m - m)
        p_d = jnp.exp2(s_d - m)
        acc = (_dot(vmt_ref[vsl, :], p_m.astype(BF16))
               + _dot(vt_ref[vsl, q0:q0 + tq], p_d.astype(BF16)))
        if s_o is not None:
            p_o = jnp.exp2(s_o - m)
            acc = acc + _dot(vt_ref[vsl, 0:q0], p_o.astype(BF16))
        return acc[:V_DIM] * (1.0 / acc[V_DIM:V_DIM + 1])

    order = [(i, h) for i in reversed(range(s_len // tq)) for h in (0, 1)]
    outs = {}
    pending = [scores(*order[n]) for n in range(min(ATTN_LOOKAHEAD, len(order)))]
    for n, (i, h) in enumerate(order):
        if n + ATTN_LOOKAHEAD < len(order):
            pending.append(scores(*order[n + ATTN_LOOKAHEAD]))
        outs[h] = weighted_values(i, h, *pending.pop(0))
        if h == 1:
            ot = jnp.concatenate([outs[0], outs[1]], axis=0)
            o_ref[i * tq:(i + 1) * tq, :] = ot.T.astype(o_ref.dtype)


def _attention(qt, k, vt, km, vmt, tq):
    b, s, _ = k.shape
    n_pairs = N_HEADS // 2
    return pl.pallas_call(
        functools.partial(_attn_kernel, tq=tq),
        grid=(b, n_pairs),
        in_specs=[
            pl.BlockSpec((None, 2 * HEAD_W, s), lambda bi, hp: (bi, hp, 0)),
            pl.BlockSpec((None, s, 2 * HEAD_W), lambda bi, hp: (bi, 0, hp)),
            pl.BlockSpec((None, 2 * V_ROWS, s), lambda bi, hp: (bi, hp, 0)),
            pl.BlockSpec((N_META, 2 * HEAD_W), lambda bi, hp: (0, hp)),
            pl.BlockSpec((2 * V_ROWS, N_META), lambda bi, hp: (hp, 0)),
        ],
        out_specs=pl.BlockSpec((None, s, 2 * V_DIM), lambda bi, hp: (bi, 0, hp)),
        out_shape=jax.ShapeDtypeStruct((b, s, N_HEADS * V_DIM), BF16),
        compiler_params=pltpu.CompilerParams(
            dimension_semantics=("parallel", "parallel"),
            vmem_limit_bytes=VMEM_LIMIT),
        name="attention",
    )(qt, k, vt, km, vmt)


CONV_PAD = 32
CONV_RB = 64
FF_CHUNK = 256
FF_GROUP = 4


def _tail_kernel(x_ref, o_ref, sga_ref, sgb_ref, u_ref, halo_ref, mhalo_ref,
                 cw_ref, cb_ref, lng_ref, lnb_ref, wao_ref, wco_ref, wout_ref,
                 gpost_ref, gfpre_ref, wfin_ref, wfout_ref, gfpost_ref,
                 out_ref, buf_ref, mix_ref, act_ref, *, tiles_per_seq, n_tiles):
    s = pl.program_id(0)
    tm, d = u_ref.shape
    d_ff = wfout_ref.shape[0]

    @pl.when(s == 0)
    def _():
        mix_ref[...] = jnp.zeros(mix_ref.shape, mix_ref.dtype)

    conv_tile = jnp.minimum(s, n_tiles - 1)
    first_in_seq = conv_tile % tiles_per_seq == 0
    row0 = jnp.minimum(s, 0)

    hist = jnp.where(first_in_seq, mhalo_ref[...], halo_ref[...]).astype(F32)
    for cb in range(d // LANES):
        csl = slice(cb * LANES, (cb + 1) * LANES)
        buf_ref[cb, 0:CONV_PAD, :] = hist[:, csl]
        buf_ref[cb, CONV_PAD:CONV_PAD + tm, :] = u_ref[:, csl].astype(F32)

    base = CONV_PAD - (CONV_K - 1)
    pieces = [(cb, r0) for cb in range(d // LANES) for r0 in range(0, tm, CONV_RB)]
    n_pieces = len(pieces)
    n_chunks = d_ff // FF_CHUNK
    n_groups = -(-n_chunks // FF_GROUP)
    wide, narrow = d // FF_CHUNK, 1
    budget = (2 * n_chunks - 1) * narrow + (n_groups - 1) * wide
    spent = [0]
    yc_c0, hb_c0 = 0, d
    gate_row = pl.multiple_of(row0, 16)

    def gated(c0):
        return mix_ref[pl.ds(gate_row, tm), c0:c0 + d]

    def conv_pieces(cost):
        lo = min(spent[0], budget) * n_pieces // budget
        spent[0] += cost
        hi = min(spent[0], budget) * n_pieces // budget
        for _ in range(hi - lo):
            cb, r0 = pieces.pop(0)
            csl = slice(cb * LANES, (cb + 1) * LANES)
            acc = jnp.broadcast_to(cb_ref[:, csl], (CONV_RB, LANES))
            for res in range(8):
                taps = list(range(res, CONV_K, 8))
                n_rows = CONV_RB + 8 * (len(taps) - 1)
                win = buf_ref[cb, pl.ds(row0 + (r0 + base + res), n_rows), :]
                for t, kk in enumerate(taps):
                    acc = acc + cw_ref[kk:kk + 1, csl] * win[8 * t:8 * t + CONV_RB]
            mix_ref[r0:r0 + CONV_RB, yc_c0 + cb * LANES:yc_c0 + (cb + 1) * LANES] = acc.astype(BF16)

    yc = mix_ref[:, yc_c0:yc_c0 + d].astype(F32)
    mu = jnp.mean(yc, axis=-1, keepdims=True)
    xc = yc - mu
    var = jnp.mean(xc * xc, axis=-1, keepdims=True)
    z = xc * lax.rsqrt(var + EPS) * lng_ref[...] + lnb_ref[...]
    z = (z * _sigmoid(z)).astype(BF16)

    y_a = _dot(o_ref[...], wao_ref[...])
    y_b = _dot(z, wco_ref[...])
    merged = sga_ref[...].astype(F32) * y_a + sgb_ref[...].astype(F32) * y_b
    mo = _dot(merged.astype(BF16), wout_ref[...])
    x1 = x_ref[...] + _rms(mo, gpost_ref[...])

    mix_ref[:, hb_c0:hb_c0 + d] = _rms(x1, gfpre_ref[...]).astype(BF16)
    f = None
    for grp in range(n_groups):
        g0 = grp * FF_GROUP * FF_CHUNK
        g1 = min(g0 + FF_GROUP * FF_CHUNK, d_ff)
        for c0 in range(g0, g1, FF_CHUNK):
            g = _dot(gated(hb_c0), wfin_ref[:, c0:c0 + FF_CHUNK])
            conv_pieces(narrow)
            up = _dot(gated(hb_c0), wfin_ref[:, d_ff + c0:d_ff + c0 + FF_CHUNK])
            conv_pieces(narrow)
            act_ref[:, c0:c0 + FF_CHUNK] = (g * _sigmoid(g) * up).astype(BF16)
        part = _dot(act_ref[:, g0:g1], wfout_ref[g0:g1, :])
        conv_pieces(wide)
        f = part if f is None else f + part
    assert not pieces
    out_ref[...] = x1 + _rms(f, gfpost_ref[...])


def _tail(x2d, o2d, sga, sgb, u2d, mhalo, weights, tm, seq):
    rows, d = x2d.shape
    d_ff = weights[10].shape[0]
    n_tiles = rows // tm
    halo_per_tile = tm // CONV_PAD
    lag_spec = pl.BlockSpec((tm, d), lambda s: (jnp.maximum(s - 1, 0), 0))
    conv_tile = lambda s: jnp.minimum(s, n_tiles - 1)
    return pl.pallas_call(
        functools.partial(_tail_kernel, tiles_per_seq=seq // tm, n_tiles=n_tiles),
        grid=(n_tiles + 1,),
        in_specs=[lag_spec] * 4 + [
            pl.BlockSpec((tm, d), lambda s: (conv_tile(s), 0)),
            pl.BlockSpec((CONV_PAD, d),
                         lambda s: (jnp.maximum(conv_tile(s) * halo_per_tile - 1, 0), 0)),
        ] + [_const_spec(mhalo.shape)] + [_const_spec(w.shape) for w in weights],
        out_specs=lag_spec,
        out_shape=jax.ShapeDtypeStruct((rows, d), F32),
        scratch_shapes=[pltpu.VMEM((d // LANES, tm + CONV_PAD, LANES), F32),
                        pltpu.VMEM((tm, 2 * d), BF16),
                        pltpu.VMEM((tm, d_ff), BF16)],
        compiler_params=pltpu.CompilerParams(
            dimension_semantics=("arbitrary",), vmem_limit_bytes=VMEM_LIMIT),
        name="tail",
    )(x2d, o2d, sga, sgb, u2d, u2d, mhalo, *weights)


def _rope_tables(length):
    pos = np.arange(length, dtype=np.float32)
    inv = np.float32(ROPE_BASE) ** (-np.arange(0, QK_ROPE, 2, dtype=np.float32) / QK_ROPE)
    ang = (pos[:, None] * inv[None, :]).astype(np.float32)
    cos, sin = np.cos(ang), np.sin(ang)
    zeros = lambda w: np.zeros((length, w), np.float32)
    ktail = LANES - QK_ROPE
    ck = np.concatenate([cos, cos, zeros(ktail)], axis=1)
    sk_lo = np.concatenate([-sin, zeros(HALF_ROPE + ktail)], axis=1)
    sk_hi = np.concatenate([zeros(HALF_ROPE), sin, zeros(ktail)], axis=1)
    tabs = np.stack([ck, sk_lo, sk_hi]).astype(np.float32)
    tabs_t = (np.stack([cos.T, sin.T]) * np.float32(Q_SCALE)).astype(np.float32)
    return tabs, tabs_t


def kernel(x, meta, mix_pre_g, w_in, q_norm_g, w_uq, kv_norm_g, w_ukv, w_attn_o,
           conv_w, conv_b, conv_ln_g, conv_ln_b, w_conv_o, w_out, mix_post_g,
           ffn_pre_g, w_ffn_in, w_ffn_out, ffn_post_g):
    assert w_in.shape[0] == 1, "single-layer block"
    b, s, d = x.shape
    row = lambda g: g[0][None, :].astype(F32)

    w_in0 = w_in[0]
    off_glu = Q_RANK + KV_RANK + QK_ROPE
    wsm = jnp.pad(w_in0[:, :off_glu], ((0, 0), (0, 4 * LANES - off_glu))).astype(BF16)
    wbig = w_in0[:, off_glu:].astype(BF16)

    qk_dim = QK_NOPE + QK_ROPE
    wq = w_uq[0].reshape(Q_RANK, N_HEADS, qk_dim)
    wq = jnp.pad(wq, ((0, 0), (0, 0), (0, HEAD_W - qk_dim)))
    wqt = wq.reshape(Q_RANK, N_HEADS * HEAD_W).T.astype(BF16)

    wkv = w_ukv[0].reshape(KV_RANK, N_HEADS, QK_NOPE + V_DIM)
    wk_nope = jnp.pad(wkv[:, :, :QK_NOPE], ((0, 0), (0, 0), (0, HEAD_W - QK_NOPE)))
    place = np.zeros((LANES, N_HEADS, HEAD_W), np.float32)
    jj = np.arange(QK_ROPE)
    place[jj, :, QK_NOPE + jj] = 1.0
    wk = jnp.concatenate([wk_nope, jnp.asarray(place)], axis=0)
    wk = wk.reshape(KV_RANK + LANES, N_HEADS * HEAD_W).astype(BF16)
    wvt = wkv[:, :, QK_NOPE:].reshape(KV_RANK, N_HEADS * V_DIM).T.astype(BF16)

    in_weights = (row(mix_pre_g), wsm, wbig, row(q_norm_g), wqt, row(kv_norm_g), wk, wvt)

    tabs, tabs_t = _rope_tables(N_META + s)

    tm = 256
    x2d = x.reshape(b * s, d)
    qt, k, vt, u, sga, sgb = _inproj(x2d, tabs[:, N_META:], tabs_t[:, :, N_META:],
                                     in_weights, tm, s)
    _, km, vmt, um, _, _ = _inproj(meta.astype(F32), tabs[:, :N_META],
                                   tabs_t[:, :, :N_META], in_weights, N_META, N_META)

    o = _attention(qt, k.reshape(b, s, -1), vt, km, vmt[0], tq=256)

    mhalo = jnp.concatenate([jnp.zeros((CONV_PAD - N_META, d), BF16), um], axis=0)
    tail_weights = (conv_w[0].astype(F32), row(conv_b), row(conv_ln_g), row(conv_ln_b),
                    w_attn_o[0].astype(BF16), w_conv_o[0].astype(BF16),
                    w_out[0].astype(BF16), row(mix_post_g), row(ffn_pre_g),
                    w_ffn_in[0].astype(BF16), w_ffn_out[0].astype(BF16), row(ffn_post_g))
    out = _tail(x2d, o.reshape(b * s, -1), sga, sgb, u, mhalo, tail_weights, tm, s)
    return out.reshape(b, s, d)
```

```python
import functools
import math

import jax
import jax.numpy as jnp
import numpy as np
from jax import lax
from jax.experimental import pallas as pl
from jax.experimental.pallas import tpu as pltpu

CHUNK = 64
N_META = 16
N_HEADS = 16
QK_NOPE = 64
QK_ROPE = 32
V_DIM = 64
Q_RANK = 256
KV_RANK = 128
ROPE_BASE = 10000.0
CONV_K = 31
EPS = 1e-6
NEG_INF = -1e30

LANES = 128
VMEM_LIMIT = 56 * 1024 * 1024

HEAD_W = LANES
HALF_ROPE = QK_ROPE // 2
V_ROWS = V_DIM + 16
Q_SCALE = math.log2(math.e) / math.sqrt(QK_NOPE + QK_ROPE)

BF16 = jnp.bfloat16
F32 = jnp.float32


def _rms(x, g):
    ms = jnp.mean(x * x, axis=-1, keepdims=True)
    return x * lax.rsqrt(ms + EPS) * g


def _sigmoid(x):
    return 0.5 * (jnp.tanh(0.5 * x) + 1.0)


def _dot(a, b):
    return jnp.dot(a, b, preferred_element_type=F32)


def _dot_t(a, b):
    return lax.dot_general(a, b, (((1,), (1,)), ((), ())),
                           preferred_element_type=F32)


def _rope(g, c, s_lo, s_hi):
    up = pltpu.roll(g, LANES - HALF_ROPE, 1)
    dn = pltpu.roll(g, HALF_ROPE, 1)
    return g * c + up * s_lo + dn * s_hi


def _inproj_kernel(x_ref, tab_ref, tabt_ref, gpre_ref, wsm_ref, wbig_ref, gq_ref,
                   wqt_ref, gkv_ref, wk_ref, wvt_ref,
                   qt_ref, k_ref, vt_ref, u_ref, sga_ref, sgb_ref):
    d = x_ref.shape[1]
    xb = _rms(x_ref[...], gpre_ref[...]).astype(BF16)

    small = _dot(xb, wsm_ref[...])
    a = _dot(xb, wbig_ref[:, 0:d])
    g = _dot(xb, wbig_ref[:, d:2 * d])
    u_ref[...] = (a * _sigmoid(g)).astype(BF16)
    qn = _rms(small[:, :Q_RANK], gq_ref[...]).astype(BF16)
    kvn = _rms(small[:, Q_RANK:Q_RANK + KV_RANK], gkv_ref[...]).astype(BF16)
    kr = _rope(small[:, Q_RANK + KV_RANK:], tab_ref[0], tab_ref[1], tab_ref[2])

    kin = jnp.concatenate([kvn, kr.astype(BF16)], axis=1)
    k_ref[...] = _dot(kin, wk_ref[...]).astype(BF16)
    vt = _dot_t(wvt_ref[...], kvn).astype(BF16)
    for h in range(N_HEADS):
        vt_ref[h * V_ROWS:h * V_ROWS + V_DIM, :] = vt[h * V_DIM:(h + 1) * V_DIM]
        vt_ref[h * V_ROWS + V_DIM:(h + 1) * V_ROWS, :] = jnp.ones(
            (V_ROWS - V_DIM, vt.shape[1]), BF16)

    qt = _dot_t(wqt_ref[...], qn)
    cos_t, sin_t = tabt_ref[0], tabt_ref[1]
    scale = Q_SCALE
    for h in range(N_HEADS):
        r0 = h * HEAD_W
        lo = qt[r0 + QK_NOPE:r0 + QK_NOPE + HALF_ROPE]
        hi = qt[r0 + QK_NOPE + HALF_ROPE:r0 + QK_NOPE + QK_ROPE]
        qt_ref[r0:r0 + QK_NOPE, :] = (qt[r0:r0 + QK_NOPE] * scale).astype(BF16)
        qt_ref[r0 + QK_NOPE:r0 + QK_NOPE + QK_ROPE, :] = jnp.concatenate(
            [lo * cos_t - hi * sin_t, hi * cos_t + lo * sin_t], axis=0).astype(BF16)
        qt_ref[r0 + QK_NOPE + QK_ROPE:r0 + HEAD_W, :] = jnp.zeros(
            (HEAD_W - QK_NOPE - QK_ROPE, qt.shape[1]), BF16)

    sga_ref[...] = _sigmoid(_dot(xb, wbig_ref[:, 2 * d:3 * d])).astype(BF16)
    sgb_ref[...] = _sigmoid(_dot(xb, wbig_ref[:, 3 * d:4 * d])).astype(BF16)


def _const_spec(shape):
    nd = len(shape)
    return pl.BlockSpec(shape, lambda *_: (0,) * nd, pipeline_mode=pl.Buffered(1))


def _inproj(x2d, tabs, tabs_t, weights, tm, seq):
    rows, d = x2d.shape
    gpre, wsm, wbig, gq, wqt, gkv, wk, wvt = weights
    n_q = wqt.shape[0]
    n_v = N_HEADS * V_ROWS
    tps = seq // tm
    row_spec = lambda w: pl.BlockSpec((tm, w), lambda i: (i, 0))
    col_spec = lambda h: pl.BlockSpec((None, h, tm), lambda i: (i // tps, 0, i % tps))
    out_shapes = [
        jax.ShapeDtypeStruct((rows // seq, n_q, seq), BF16),
        jax.ShapeDtypeStruct((rows, n_q), BF16),
        jax.ShapeDtypeStruct((rows // seq, n_v, seq), BF16),
        jax.ShapeDtypeStruct((rows, d), BF16),
        jax.ShapeDtypeStruct((rows, d), BF16),
        jax.ShapeDtypeStruct((rows, d), BF16),
    ]
    return pl.pallas_call(
        _inproj_kernel,
        grid=(rows // tm,),
        in_specs=[
            row_spec(d),
            pl.BlockSpec((3, tm, LANES), lambda i: (0, i % tps, 0)),
            pl.BlockSpec((2, HALF_ROPE, tm), lambda i: (0, 0, i % tps)),
            _const_spec(gpre.shape), _const_spec(wsm.shape),
            _const_spec(wbig.shape), _const_spec(gq.shape),
            _const_spec(wqt.shape), _const_spec(gkv.shape),
            _const_spec(wk.shape), _const_spec(wvt.shape),
        ],
        out_specs=[col_spec(n_q), row_spec(n_q), col_spec(n_v),
                   row_spec(d), row_spec(d), row_spec(d)],
        out_shape=out_shapes,
        compiler_params=pltpu.CompilerParams(
            dimension_semantics=("parallel",), vmem_limit_bytes=VMEM_LIMIT),
        name="inproj",
    )(x2d, tabs, tabs_t, gpre, wsm, wbig, gq, wqt, gkv, wk, wvt)


ATTN_LOOKAHEAD = 3
ATTN_HEADS = 4


def _attn_kernel(qt_ref, k_ref, vt_ref, km_ref, vmt_ref, o_ref, *, tq):
    s_len = k_ref.shape[0]
    key_chunk = lax.broadcasted_iota(jnp.int32, (tq, tq), 0) // CHUNK
    qry_chunk = lax.broadcasted_iota(jnp.int32, (tq, tq), 1) // CHUNK
    diag_ok = key_chunk <= qry_chunk

    def scores(i, h):
        q0 = i * tq
        ksl = slice(h * HEAD_W, (h + 1) * HEAD_W)
        qt = qt_ref[ksl, q0:q0 + tq]
        s_m = _dot(km_ref[:, ksl], qt)
        s_d = jnp.where(diag_ok, _dot(k_ref[q0:q0 + tq, ksl], qt), NEG_INF)
        m = jnp.maximum(jnp.max(s_m, axis=0, keepdims=True),
                        jnp.max(s_d, axis=0, keepdims=True))
        s_o = None
        if i > 0:
            s_o = _dot(k_ref[0:q0, ksl], qt)
            m = jnp.maximum(m, jnp.max(s_o, axis=0, keepdims=True))
        return s_m, s_d, s_o, m

    def weighted_values(i, h, s_m, s_d, s_o, m):
        q0 = i * tq
        vsl = slice(h * V_ROWS, (h + 1) * V_ROWS)
        p_m = jnp.exp2(s_m - m)
        p_d = jnp.exp2(s_d - m)
        acc = (_dot(vmt_ref[vsl, :], p_m.astype(BF16))
               + _dot(vt_ref[vsl, q0:q0 + tq], p_d.astype(BF16)))
        if s_o is not None:
            p_o = jnp.exp2(s_o - m)
            acc = acc + _dot(vt_ref[vsl, 0:q0], p_o.astype(BF16))
        return acc[:V_DIM] * (1.0 / acc[V_DIM:V_DIM + 1])

    order = [(i, h) for i in reversed(range(s_len // tq)) for h in range(ATTN_HEADS)]
    outs = {}
    pending = [scores(*order[n]) for n in range(min(ATTN_LOOKAHEAD, len(order)))]
    for n, (i, h) in enumerate(order):
        if n + ATTN_LOOKAHEAD < len(order):
            pending.append(scores(*order[n + ATTN_LOOKAHEAD]))
        outs[h] = weighted_values(i, h, *pending.pop(0))
        if h % 2 == 1:
            ot = jnp.concatenate([outs[h - 1], outs[h]], axis=0)
            o_ref[i * tq:(i + 1) * tq, (h - 1) * V_DIM:(h + 1) * V_DIM] = ot.T.astype(o_ref.dtype)


def _attention(qt, k, vt, km, vmt, tq):
    b, s, _ = k.shape
    g = ATTN_HEADS
    return pl.pallas_call(
        functools.partial(_attn_kernel, tq=tq),
        grid=(b, N_HEADS // g),
        in_specs=[
            pl.BlockSpec((None, g * HEAD_W, s), lambda bi, hp: (bi, hp, 0)),
            pl.BlockSpec((None, s, g * HEAD_W), lambda bi, hp: (bi, 0, hp)),
            pl.BlockSpec((None, g * V_ROWS, s), lambda bi, hp: (bi, hp, 0)),
            pl.BlockSpec((N_META, g * HEAD_W), lambda bi, hp: (0, hp)),
            pl.BlockSpec((g * V_ROWS, N_META), lambda bi, hp: (hp, 0)),
        ],
        out_specs=pl.BlockSpec((None, s, g * V_DIM), lambda bi, hp: (bi, 0, hp)),
        out_shape=jax.ShapeDtypeStruct((b, s, N_HEADS * V_DIM), BF16),
        compiler_params=pltpu.CompilerParams(
            dimension_semantics=("parallel", "parallel"),
            vmem_limit_bytes=VMEM_LIMIT),
        name="attention",
    )(qt, k, vt, km, vmt)


CONV_PAD = 32
CONV_RB = 64
FF_CHUNK = 256
FF_GROUP = 4


def _tail_kernel(x_ref, o_ref, sga_ref, sgb_ref, u_ref, halo_ref, mhalo_ref,
                 cw_ref, cb_ref, lng_ref, lnb_ref, wao_ref, wco_ref, wout_ref,
                 gpost_ref, gfpre_ref, wfin_ref, wfout_ref, gfpost_ref,
                 out_ref, buf_ref, mix_ref, act_ref, *, tiles_per_seq, n_tiles):
    s = pl.program_id(0)
    tm, d = u_ref.shape
    d_ff = wfout_ref.shape[0]

    @pl.when(s == 0)
    def _():
        mix_ref[...] = jnp.zeros(mix_ref.shape, mix_ref.dtype)

    conv_tile = jnp.minimum(s, n_tiles - 1)
    first_in_seq = conv_tile % tiles_per_seq == 0
    row0 = jnp.minimum(s, 0)

    hist = jnp.where(first_in_seq, mhalo_ref[...], halo_ref[...]).astype(F32)
    for cb in range(d // LANES):
        csl = slice(cb * LANES, (cb + 1) * LANES)
        buf_ref[cb, 0:CONV_PAD, :] = hist[:, csl]
        buf_ref[cb, CONV_PAD:CONV_PAD + tm, :] = u_ref[:, csl].astype(F32)

    base = CONV_PAD - (CONV_K - 1)
    pieces = [(cb, r0) for cb in range(d // LANES) for r0 in range(0, tm, CONV_RB)]
    n_pieces = len(pieces)
    n_chunks = d_ff // FF_CHUNK
    n_groups = -(-n_chunks // FF_GROUP)
    wide, narrow = d // FF_CHUNK, 1
    budget = (2 * n_chunks - 1) * narrow + (n_groups - 1) * wide
    spent = [0]
    yc_c0, hb_c0 = 0, d
    gate_row = pl.multiple_of(row0, 16)

    def gated(c0):
        return mix_ref[pl.ds(gate_row, tm), c0:c0 + d]

    def conv_pieces(cost):
        lo = min(spent[0], budget) * n_pieces // budget
        spent[0] += cost
        hi = min(spent[0], budget) * n_pieces // budget
        for _ in range(hi - lo):
            cb, r0 = pieces.pop(0)
            csl = slice(cb * LANES, (cb + 1) * LANES)
            acc = jnp.broadcast_to(cb_ref[:, csl], (CONV_RB, LANES))
            for res in range(8):
                taps = list(range(res, CONV_K, 8))
                n_rows = CONV_RB + 8 * (len(taps) - 1)
                win = buf_ref[cb, pl.ds(row0 + (r0 + base + res), n_rows), :]
                for t, kk in enumerate(taps):
                    acc = acc + cw_ref[kk:kk + 1, csl] * win[8 * t:8 * t + CONV_RB]
            mix_ref[r0:r0 + CONV_RB, yc_c0 + cb * LANES:yc_c0 + (cb + 1) * LANES] = acc.astype(BF16)

    yc = mix_ref[:, yc_c0:yc_c0 + d].astype(F32)
    mu = jnp.mean(yc, axis=-1, keepdims=True)
    xc = yc - mu
    var = jnp.mean(xc * xc, axis=-1, keepdims=True)
    z = xc * lax.rsqrt(var + EPS) * lng_ref[...] + lnb_ref[...]
    z = (z * _sigmoid(z)).astype(BF16)

    y_a = _dot(o_ref[...], wao_ref[...])
    y_b = _dot(z, wco_ref[...])
    merged = sga_ref[...].astype(F32) * y_a + sgb_ref[...].astype(F32) * y_b
    mo = _dot(merged.astype(BF16), wout_ref[...])
    x1 = x_ref[...] + _rms(mo, gpost_ref[...])

    mix_ref[:, hb_c0:hb_c0 + d] = _rms(x1, gfpre_ref[...]).astype(BF16)
    f = None
    for grp in range(n_groups):
        g0 = grp * FF_GROUP * FF_CHUNK
        g1 = min(g0 + FF_GROUP * FF_CHUNK, d_ff)
        for c0 in range(g0, g1, FF_CHUNK):
            g = _dot(gated(hb_c0), wfin_ref[:, c0:c0 + FF_CHUNK])
            conv_pieces(narrow)
            up = _dot(gated(hb_c0), wfin_ref[:, d_ff + c0:d_ff + c0 + FF_CHUNK])
            conv_pieces(narrow)
            act_ref[:, c0:c0 + FF_CHUNK] = (g * _sigmoid(g) * up).astype(BF16)
        part = _dot(act_ref[:, g0:g1], wfout_ref[g0:g1, :])
        conv_pieces(wide)
        f = part if f is None else f + part
    assert not pieces
    out_ref[...] = x1 + _rms(f, gfpost_ref[...])


def _tail(x2d, o2d, sga, sgb, u2d, mhalo, weights, tm, seq):
    rows, d = x2d.shape
    d_ff = weights[10].shape[0]
    n_tiles = rows // tm
    halo_per_tile = tm // CONV_PAD
    lag_spec = pl.BlockSpec((tm, d), lambda s: (jnp.maximum(s - 1, 0), 0))
    conv_tile = lambda s: jnp.minimum(s, n_tiles - 1)
    return pl.pallas_call(
        functools.partial(_tail_kernel, tiles_per_seq=seq // tm, n_tiles=n_tiles),
        grid=(n_tiles + 1,),
        in_specs=[lag_spec] * 4 + [
            pl.BlockSpec((tm, d), lambda s: (conv_tile(s), 0)),
            pl.BlockSpec((CONV_PAD, d),
                         lambda s: (jnp.maximum(conv_tile(s) * halo_per_tile - 1, 0), 0)),
        ] + [_const_spec(mhalo.shape)] + [_const_spec(w.shape) for w in weights],
        out_specs=lag_spec,
        out_shape=jax.ShapeDtypeStruct((rows, d), F32),
        scratch_shapes=[pltpu.VMEM((d // LANES, tm + CONV_PAD, LANES), F32),
                        pltpu.VMEM((tm, 2 * d), BF16),
                        pltpu.VMEM((tm, d_ff), BF16)],
        compiler_params=pltpu.CompilerParams(
            dimension_semantics=("arbitrary",), vmem_limit_bytes=VMEM_LIMIT),
        name="tail",
    )(x2d, o2d, sga, sgb, u2d, u2d, mhalo, *weights)


def _rope_tables(length):
    pos = np.arange(length, dtype=np.float32)
    inv = np.float32(ROPE_BASE) ** (-np.arange(0, QK_ROPE, 2, dtype=np.float32) / QK_ROPE)
    ang = (pos[:, None] * inv[None, :]).astype(np.float32)
    cos, sin = np.cos(ang), np.sin(ang)
    zeros = lambda w: np.zeros((length, w), np.float32)
    ktail = LANES - QK_ROPE
    ck = np.concatenate([cos, cos, zeros(ktail)], axis=1)
    sk_lo = np.concatenate([-sin, zeros(HALF_ROPE + ktail)], axis=1)
    sk_hi = np.concatenate([zeros(HALF_ROPE), sin, zeros(ktail)], axis=1)
    tabs = np.stack([ck, sk_lo, sk_hi]).astype(np.float32)
    tabs_t = (np.stack([cos.T, sin.T]) * np.float32(Q_SCALE)).astype(np.float32)
    return tabs, tabs_t


def kernel(x, meta, mix_pre_g, w_in, q_norm_g, w_uq, kv_norm_g, w_ukv, w_attn_o,
           conv_w, conv_b, conv_ln_g, conv_ln_b, w_conv_o, w_out, mix_post_g,
           ffn_pre_g, w_ffn_in, w_ffn_out, ffn_post_g):
    assert w_in.shape[0] == 1, "single-layer block"
    b, s, d = x.shape
    row = lambda g: g[0][None, :].astype(F32)

    w_in0 = w_in[0]
    off_glu = Q_RANK + KV_RANK + QK_ROPE
    wsm = jnp.pad(w_in0[:, :off_glu], ((0, 0), (0, 4 * LANES - off_glu))).astype(BF16)
    wbig = w_in0[:, off_glu:].astype(BF16)

    qk_dim = QK_NOPE + QK_ROPE
    wq = w_uq[0].reshape(Q_RANK, N_HEADS, qk_dim)
    wq = jnp.pad(wq, ((0, 0), (0, 0), (0, HEAD_W - qk_dim)))
    wqt = wq.reshape(Q_RANK, N_HEADS * HEAD_W).T.astype(BF16)

    wkv = w_ukv[0].reshape(KV_RANK, N_HEADS, QK_NOPE + V_DIM)
    wk_nope = jnp.pad(wkv[:, :, :QK_NOPE], ((0, 0), (0, 0), (0, HEAD_W - QK_NOPE)))
    place = np.zeros((LANES, N_HEADS, HEAD_W), np.float32)
    jj = np.arange(QK_ROPE)
    place[jj, :, QK_NOPE + jj] = 1.0
    wk = jnp.concatenate([wk_nope, jnp.asarray(place)], axis=0)
    wk = wk.reshape(KV_RANK + LANES, N_HEADS * HEAD_W).astype(BF16)
    wvt = wkv[:, :, QK_NOPE:].reshape(KV_RANK, N_HEADS * V_DIM).T.astype(BF16)

    in_weights = (row(mix_pre_g), wsm, wbig, row(q_norm_g), wqt, row(kv_norm_g), wk, wvt)

    tabs, tabs_t = _rope_tables(N_META + s)

    tm = 256
    x2d = x.reshape(b * s, d)
    qt, k, vt, u, sga, sgb = _inproj(x2d, tabs[:, N_META:], tabs_t[:, :, N_META:],
                                     in_weights, 2 * tm, s)
    _, km, vmt, um, _, _ = _inproj(meta.astype(F32), tabs[:, :N_META],
                                   tabs_t[:, :, :N_META], in_weights, N_META, N_META)

    o = _attention(qt, k.reshape(b, s, -1), vt, km, vmt[0], tq=256)

    mhalo = jnp.concatenate([jnp.zeros((CONV_PAD - N_META, d), BF16), um], axis=0)
    tail_weights = (conv_w[0].astype(F32), row(conv_b), row(conv_ln_g), row(conv_ln_b),
                    w_attn_o[0].astype(BF16), w_conv_o[0].astype(BF16),
                    w_out[0].astype(BF16), row(mix_post_g), row(ffn_pre_g),
                    w_ffn_in[0].astype(BF16), w_ffn_out[0].astype(BF16), row(ffn_post_g))
    out = _tail(x2d, o.reshape(b * s, -1), sga, sgb, u, mhalo, tail_weights, tm, s)
    return out.reshape(b, s, d)
```

```python
import functools
import math

import jax
import jax.numpy as jnp
import numpy as np
from jax import lax
from jax.experimental import pallas as pl
from jax.experimental.pallas import tpu as pltpu

CHUNK = 64
N_META = 16
N_HEADS = 16
QK_NOPE = 64
QK_ROPE = 32
V_DIM = 64
Q_RANK = 256
KV_RANK = 128
ROPE_BASE = 10000.0
CONV_K = 31
EPS = 1e-6
NEG_INF = -1e30

LANES = 128
VMEM_LIMIT = 56 * 1024 * 1024

HEAD_W = LANES
HALF_ROPE = QK_ROPE // 2
V_ROWS = V_DIM + 16
Q_SCALE = math.log2(math.e) / math.sqrt(QK_NOPE + QK_ROPE)

BF16 = jnp.bfloat16
F32 = jnp.float32


def _rms(x, g):
    ms = jnp.mean(x * x, axis=-1, keepdims=True)
    return x * lax.rsqrt(ms + EPS) * g


def _sigmoid(x):
    return 0.5 * (jnp.tanh(0.5 * x) + 1.0)


def _dot(a, b):
    return jnp.dot(a, b, preferred_element_type=F32)


def _dot_t(a, b):
    return lax.dot_general(a, b, (((1,), (1,)), ((), ())),
                           preferred_element_type=F32)


def _rope(g, c, s_lo, s_hi):
    up = pltpu.roll(g, LANES - HALF_ROPE, 1)
    dn = pltpu.roll(g, HALF_ROPE, 1)
    return g * c + up * s_lo + dn * s_hi


def _inproj_kernel(x_ref, tab_ref, tabt_ref, gpre_ref, wsm_ref, wbig_ref, gq_ref,
                   wqt_ref, gkv_ref, wk_ref, wvt_ref,
                   qt_ref, k_ref, vt_ref, u_ref, sga_ref, sgb_ref):
    d = x_ref.shape[1]
    xb = _rms(x_ref[...], gpre_ref[...]).astype(BF16)

    small = _dot(xb, wsm_ref[...])
    a = _dot(xb, wbig_ref[:, 0:d])
    g = _dot(xb, wbig_ref[:, d:2 * d])
    u_ref[...] = (a * _sigmoid(g)).astype(BF16)
    qn = _rms(small[:, :Q_RANK], gq_ref[...]).astype(BF16)
    kvn = _rms(small[:, Q_RANK:Q_RANK + KV_RANK], gkv_ref[...]).astype(BF16)
    kr = _rope(small[:, Q_RANK + KV_RANK:], tab_ref[0], tab_ref[1], tab_ref[2])

    kin = jnp.concatenate([kvn, kr.astype(BF16)], axis=1)
    k_ref[...] = _dot(kin, wk_ref[...]).astype(BF16)
    vt = _dot_t(wvt_ref[...], kvn).astype(BF16)
    for h in range(N_HEADS):
        vt_ref[h * V_ROWS:h * V_ROWS + V_DIM, :] = vt[h * V_DIM:(h + 1) * V_DIM]
        vt_ref[h * V_ROWS + V_DIM:(h + 1) * V_ROWS, :] = jnp.ones(
            (V_ROWS - V_DIM, vt.shape[1]), BF16)

    qt = _dot_t(wqt_ref[...], qn)
    cos_t, sin_t = tabt_ref[0], tabt_ref[1]
    scale = Q_SCALE
    for h in range(N_HEADS):
        r0 = h * HEAD_W
        lo = qt[r0 + QK_NOPE:r0 + QK_NOPE + HALF_ROPE]
        hi = qt[r0 + QK_NOPE + HALF_ROPE:r0 + QK_NOPE + QK_ROPE]
        qt_ref[r0:r0 + QK_NOPE, :] = (qt[r0:r0 + QK_NOPE] * scale).astype(BF16)
        qt_ref[r0 + QK_NOPE:r0 + QK_NOPE + QK_ROPE, :] = jnp.concatenate(
            [lo * cos_t - hi * sin_t, hi * cos_t + lo * sin_t], axis=0).astype(BF16)
        qt_ref[r0 + QK_NOPE + QK_ROPE:r0 + HEAD_W, :] = jnp.zeros(
            (HEAD_W - QK_NOPE - QK_ROPE, qt.shape[1]), BF16)

    sga_ref[...] = _sigmoid(_dot(xb, wbig_ref[:, 2 * d:3 * d])).astype(BF16)
    sgb_ref[...] = _sigmoid(_dot(xb, wbig_ref[:, 3 * d:4 * d])).astype(BF16)


def _const_spec(shape):
    nd = len(shape)
    return pl.BlockSpec(shape, lambda *_: (0,) * nd, pipeline_mode=pl.Buffered(1))


def _inproj(x2d, tabs, tabs_t, weights, tm, seq):
    rows, d = x2d.shape
    gpre, wsm, wbig, gq, wqt, gkv, wk, wvt = weights
    n_q = wqt.shape[0]
    n_v = N_HEADS * V_ROWS
    tps = seq // tm
    row_spec = lambda w: pl.BlockSpec((tm, w), lambda i: (i, 0))
    col_spec = lambda h: pl.BlockSpec((None, h, tm), lambda i: (i // tps, 0, i % tps))
    out_shapes = [
        jax.ShapeDtypeStruct((rows // seq, n_q, seq), BF16),
        jax.ShapeDtypeStruct((rows, n_q), BF16),
        jax.ShapeDtypeStruct((rows // seq, n_v, seq), BF16),
        jax.ShapeDtypeStruct((rows, d), BF16),
        jax.ShapeDtypeStruct((rows, d), BF16),
        jax.ShapeDtypeStruct((rows, d), BF16),
    ]
    return pl.pallas_call(
        _inproj_kernel,
        grid=(rows // tm,),
        in_specs=[
            row_spec(d),
            pl.BlockSpec((3, tm, LANES), lambda i: (0, i % tps, 0)),
            pl.BlockSpec((2, HALF_ROPE, tm), lambda i: (0, 0, i % tps)),
            _const_spec(gpre.shape), _const_spec(wsm.shape),
            _const_spec(wbig.shape), _const_spec(gq.shape),
            _const_spec(wqt.shape), _const_spec(gkv.shape),
            _const_spec(wk.shape), _const_spec(wvt.shape),
        ],
        out_specs=[col_spec(n_q), row_spec(n_q), col_spec(n_v),
                   row_spec(d), row_spec(d), row_spec(d)],
        out_shape=out_shapes,
        compiler_params=pltpu.CompilerParams(
            dimension_semantics=("parallel",), vmem_limit_bytes=VMEM_LIMIT),
        name="inproj",
    )(x2d, tabs, tabs_t, gpre, wsm, wbig, gq, wqt, gkv, wk, wvt)


ATTN_LOOKAHEAD = 6
ATTN_HEADS = 4


def _attn_kernel(qt_ref, k_ref, vt_ref, km_ref, vmt_ref, o_ref, *, tq):
    s_len = k_ref.shape[0]
    key_chunk = lax.broadcasted_iota(jnp.int32, (tq, tq), 0) // CHUNK
    qry_chunk = lax.broadcasted_iota(jnp.int32, (tq, tq), 1) // CHUNK
    diag_ok = key_chunk <= qry_chunk

    items = []
    for i in reversed(range(s_len // tq)):
        for h in range(ATTN_HEADS):
            items += [(i, h, c) for c in [None] + list(range(i + 1))]

    def scores(i, h, c):
        ksl = slice(h * HEAD_W, (h + 1) * HEAD_W)
        qt = qt_ref[ksl, i * tq:(i + 1) * tq]
        if c is None:
            return _dot(km_ref[:, ksl], qt)
        s = _dot(k_ref[c * tq:(c + 1) * tq, ksl], qt)
        return jnp.where(diag_ok, s, NEG_INF) if c == i else s

    def update(state, s, h, c):
        vsl = slice(h * V_ROWS, (h + 1) * V_ROWS)
        cm = jnp.max(s, axis=0, keepdims=True)
        m_new = cm if state is None else jnp.maximum(state[0], cm)
        p = jnp.exp2(s - m_new).astype(BF16)
        v = vmt_ref[vsl, :] if c is None else vt_ref[vsl, c * tq:(c + 1) * tq]
        pv = _dot(v, p)
        if state is None:
            return m_new, pv
        return m_new, jnp.exp2(state[0] - m_new) * state[1] + pv

    pending = [scores(*items[n]) for n in range(min(ATTN_LOOKAHEAD, len(items)))]
    states, outs = {}, {}
    for n, (i, h, c) in enumerate(items):
        if n + ATTN_LOOKAHEAD < len(items):
            pending.append(scores(*items[n + ATTN_LOOKAHEAD]))
        states[h] = update(states.get(h), pending.pop(0), h, c)
        if c == i:
            acc = states.pop(h)[1]
            outs[h] = acc[:V_DIM] * (1.0 / acc[V_DIM:V_DIM + 1])
            if h % 2 == 1:
                ot = jnp.concatenate([outs[h - 1], outs[h]], axis=0)
                o_ref[i * tq:(i + 1) * tq, (h - 1) * V_DIM:(h + 1) * V_DIM] = ot.T.astype(o_ref.dtype)


def _attention(qt, k, vt, km, vmt, tq):
    b, s, _ = k.shape
    g = ATTN_HEADS
    return pl.pallas_call(
        functools.partial(_attn_kernel, tq=tq),
        grid=(b, N_HEADS // g),
        in_specs=[
            pl.BlockSpec((None, g * HEAD_W, s), lambda bi, hp: (bi, hp, 0)),
            pl.BlockSpec((None, s, g * HEAD_W), lambda bi, hp: (bi, 0, hp)),
            pl.BlockSpec((None, g * V_ROWS, s), lambda bi, hp: (bi, hp, 0)),
            pl.BlockSpec((N_META, g * HEAD_W), lambda bi, hp: (0, hp)),
            pl.BlockSpec((g * V_ROWS, N_META), lambda bi, hp: (hp, 0)),
        ],
        out_specs=pl.BlockSpec((None, s, g * V_DIM), lambda bi, hp: (bi, 0, hp)),
        out_shape=jax.ShapeDtypeStruct((b, s, N_HEADS * V_DIM), BF16),
        compiler_params=pltpu.CompilerParams(
            dimension_semantics=("parallel", "parallel"),
            vmem_limit_bytes=VMEM_LIMIT),
        name="attention",
    )(qt, k, vt, km, vmt)


CONV_PAD = 32
CONV_RB = 64
FF_CHUNK = 256
FF_GROUP = 4


def _tail_kernel(x_ref, o_ref, sga_ref, sgb_ref, u_ref, halo_ref, mhalo_ref,
                 cw_ref, cb_ref, lng_ref, lnb_ref, wao_ref, wco_ref, wout_ref,
                 gpost_ref, gfpre_ref, wfin_ref, wfout_ref, gfpost_ref,
                 out_ref, buf_ref, mix_ref, act_ref, *, tiles_per_seq, n_tiles):
    s = pl.program_id(0)
    tm, d = u_ref.shape
    d_ff = wfout_ref.shape[0]

    @pl.when(s == 0)
    def _():
        mix_ref[...] = jnp.zeros(mix_ref.shape, mix_ref.dtype)

    conv_tile = jnp.minimum(s, n_tiles - 1)
    first_in_seq = conv_tile % tiles_per_seq == 0
    row0 = jnp.minimum(s, 0)

    hist = jnp.where(first_in_seq, mhalo_ref[...], halo_ref[...]).astype(F32)
    for cb in range(d // LANES):
        csl = slice(cb * LANES, (cb + 1) * LANES)
        buf_ref[cb, 0:CONV_PAD, :] = hist[:, csl]
        buf_ref[cb, CONV_PAD:CONV_PAD + tm, :] = u_ref[:, csl].astype(F32)

    base = CONV_PAD - (CONV_K - 1)
    pieces = [(cb, r0) for cb in range(d // LANES) for r0 in range(0, tm, CONV_RB)]
    n_pieces = len(pieces)
    n_chunks = d_ff // FF_CHUNK
    n_groups = -(-n_chunks // FF_GROUP)
    wide, narrow = d // FF_CHUNK, 1
    budget = (2 * n_chunks - 1) * narrow + (n_groups - 1) * wide
    spent = [0]
    yc_c0, hb_c0 = 0, d
    gate_row = pl.multiple_of(row0, 16)

    def gated(c0):
        return mix_ref[pl.ds(gate_row, tm), c0:c0 + d]

    def conv_pieces(cost):
        lo = min(spent[0], budget) * n_pieces // budget
        spent[0] += cost
        hi = min(spent[0], budget) * n_pieces // budget
        for _ in range(hi - lo):
            cb, r0 = pieces.pop(0)
            csl = slice(cb * LANES, (cb + 1) * LANES)
            acc = jnp.broadcast_to(cb_ref[:, csl], (CONV_RB, LANES))
            for res in range(8):
                taps = list(range(res, CONV_K, 8))
                n_rows = CONV_RB + 8 * (len(taps) - 1)
                win = buf_ref[cb, pl.ds(row0 + (r0 + base + res), n_rows), :]
                for t, kk in enumerate(taps):
                    acc = acc + cw_ref[kk:kk + 1, csl] * win[8 * t:8 * t + CONV_RB]
            mix_ref[r0:r0 + CONV_RB, yc_c0 + cb * LANES:yc_c0 + (cb + 1) * LANES] = acc.astype(BF16)

    yc = mix_ref[:, yc_c0:yc_c0 + d].astype(F32)
    mu = jnp.mean(yc, axis=-1, keepdims=True)
    xc = yc - mu
    var = jnp.mean(xc * xc, axis=-1, keepdims=True)
    z = xc * lax.rsqrt(var + EPS) * lng_ref[...] + lnb_ref[...]
    z = (z * _sigmoid(z)).astype(BF16)

    y_a = _dot(o_ref[...], wao_ref[...])
    y_b = _dot(z, wco_ref[...])
    merged = sga_ref[...].astype(F32) * y_a + sgb_ref[...].astype(F32) * y_b
    mo = _dot(merged.astype(BF16), wout_ref[...])
    x1 = x_ref[...] + _rms(mo, gpost_ref[...])

    mix_ref[:, hb_c0:hb_c0 + d] = _rms(x1, gfpre_ref[...]).astype(BF16)
    f = None
    for grp in range(n_groups):
        g0 = grp * FF_GROUP * FF_CHUNK
        g1 = min(g0 + FF_GROUP * FF_CHUNK, d_ff)
        for c0 in range(g0, g1, FF_CHUNK):
            g = _dot(gated(hb_c0), wfin_ref[:, c0:c0 + FF_CHUNK])
            conv_pieces(narrow)
            up = _dot(gated(hb_c0), wfin_ref[:, d_ff + c0:d_ff + c0 + FF_CHUNK])
            conv_pieces(narrow)
            act_ref[:, c0:c0 + FF_CHUNK] = (g * _sigmoid(g) * up).astype(BF16)
        part = _dot(act_ref[:, g0:g1], wfout_ref[g0:g1, :])
        conv_pieces(wide)
        f = part if f is None else f + part
    assert not pieces
    out_ref[...] = x1 + _rms(f, gfpost_ref[...])


def _tail(x2d, o2d, sga, sgb, u2d, mhalo, weights, tm, seq):
    rows, d = x2d.shape
    d_ff = weights[10].shape[0]
    n_tiles = rows // tm
    halo_per_tile = tm // CONV_PAD
    lag_spec = pl.BlockSpec((tm, d), lambda s: (jnp.maximum(s - 1, 0), 0))
    conv_tile = lambda s: jnp.minimum(s, n_tiles - 1)
    return pl.pallas_call(
        functools.partial(_tail_kernel, tiles_per_seq=seq // tm, n_tiles=n_tiles),
        grid=(n_tiles + 1,),
        in_specs=[lag_spec] * 4 + [
            pl.BlockSpec((tm, d), lambda s: (conv_tile(s), 0)),
            pl.BlockSpec((CONV_PAD, d),
                         lambda s: (jnp.maximum(conv_tile(s) * halo_per_tile - 1, 0), 0)),
        ] + [_const_spec(mhalo.shape)] + [_const_spec(w.shape) for w in weights],
        out_specs=lag_spec,
        out_shape=jax.ShapeDtypeStruct((rows, d), F32),
        scratch_shapes=[pltpu.VMEM((d // LANES, tm + CONV_PAD, LANES), F32),
                        pltpu.VMEM((tm, 2 * d), BF16),
                        pltpu.VMEM((tm, d_ff), BF16)],
        compiler_params=pltpu.CompilerParams(
            dimension_semantics=("arbitrary",), vmem_limit_bytes=VMEM_LIMIT),
        name="tail",
    )(x2d, o2d, sga, sgb, u2d, u2d, mhalo, *weights)


def _rope_tables(length):
    pos = np.arange(length, dtype=np.float32)
    inv = np.float32(ROPE_BASE) ** (-np.arange(0, QK_ROPE, 2, dtype=np.float32) / QK_ROPE)
    ang = (pos[:, None] * inv[None, :]).astype(np.float32)
    cos, sin = np.cos(ang), np.sin(ang)
    zeros = lambda w: np.zeros((length, w), np.float32)
    ktail = LANES - QK_ROPE
    ck = np.concatenate([cos, cos, zeros(ktail)], axis=1)
    sk_lo = np.concatenate([-sin, zeros(HALF_ROPE + ktail)], axis=1)
    sk_hi = np.concatenate([zeros(HALF_ROPE), sin, zeros(ktail)], axis=1)
    tabs = np.stack([ck, sk_lo, sk_hi]).astype(np.float32)
    tabs_t = (np.stack([cos.T, sin.T]) * np.float32(Q_SCALE)).astype(np.float32)
    return tabs, tabs_t


def kernel(x, meta, mix_pre_g, w_in, q_norm_g, w_uq, kv_norm_g, w_ukv, w_attn_o,
           conv_w, conv_b, conv_ln_g, conv_ln_b, w_conv_o, w_out, mix_post_g,
           ffn_pre_g, w_ffn_in, w_ffn_out, ffn_post_g):
    assert w_in.shape[0] == 1, "single-layer block"
    b, s, d = x.shape
    row = lambda g: g[0][None, :].astype(F32)

    w_in0 = w_in[0]
    off_glu = Q_RANK + KV_RANK + QK_ROPE
    wsm = jnp.pad(w_in0[:, :off_glu], ((0, 0), (0, 4 * LANES - off_glu))).astype(BF16)
    wbig = w_in0[:, off_glu:].astype(BF16)

    qk_dim = QK_NOPE + QK_ROPE
    wq = w_uq[0].reshape(Q_RANK, N_HEADS, qk_dim)
    wq = jnp.pad(wq, ((0, 0), (0, 0), (0, HEAD_W - qk_dim)))
    wqt = wq.reshape(Q_RANK, N_HEADS * HEAD_W).T.astype(BF16)

    wkv = w_ukv[0].reshape(KV_RANK, N_HEADS, QK_NOPE + V_DIM)
    wk_nope = jnp.pad(wkv[:, :, :QK_NOPE], ((0, 0), (0, 0), (0, HEAD_W - QK_NOPE)))
    place = np.zeros((LANES, N_HEADS, HEAD_W), np.float32)
    jj = np.arange(QK_ROPE)
    place[jj, :, QK_NOPE + jj] = 1.0
    wk = jnp.concatenate([wk_nope, jnp.asarray(place)], axis=0)
    wk = wk.reshape(KV_RANK + LANES, N_HEADS * HEAD_W).astype(BF16)
    wvt = wkv[:, :, QK_NOPE:].reshape(KV_RANK, N_HEADS * V_DIM).T.astype(BF16)

    in_weights = (row(mix_pre_g), wsm, wbig, row(q_norm_g), wqt, row(kv_norm_g), wk, wvt)

    tabs, tabs_t = _rope_tables(N_META + s)

    tm = 256
    x2d = x.reshape(b * s, d)
    qt, k, vt, u, sga, sgb = _inproj(x2d, tabs[:, N_META:], tabs_t[:, :, N_META:],
                                     in_weights, 2 * tm, s)
    _, km, vmt, um, _, _ = _inproj(meta.astype(F32), tabs[:, :N_META],
                                   tabs_t[:, :, :N_META], in_weights, N_META, N_META)

    o = _attention(qt, k.reshape(b, s, -1), vt, km, vmt[0], tq=256)

    mhalo = jnp.concatenate([jnp.zeros((CONV_PAD - N_META, d), BF16), um], axis=0)
    tail_weights = (conv_w[0].astype(F32), row(conv_b), row(conv_ln_g), row(conv_ln_b),
                    w_attn_o[0].astype(BF16), w_conv_o[0].astype(BF16),
                    w_out[0].astype(BF16), row(mix_post_g), row(ffn_pre_g),
                    w_ffn_in[0].astype(BF16), w_ffn_out[0].astype(BF16), row(ffn_post_g))
    out = _tail(x2d, o.reshape(b * s, -1), sga, sgb, u, mhalo, tail_weights, tm, s)
    return out.reshape(b, s, d)
```

```python
import functools
import math

import jax
import jax.numpy as jnp
import numpy as np
from jax import lax
from jax.experimental import pallas as pl
from jax.experimental.pallas import tpu as pltpu

CHUNK = 64
N_META = 16
N_HEADS = 16
QK_NOPE = 64
QK_ROPE = 32
V_DIM = 64
Q_RANK = 256
KV_RANK = 128
ROPE_BASE = 10000.0
CONV_K = 31
EPS = 1e-6
NEG_INF = -1e30

LANES = 128
VMEM_LIMIT = 56 * 1024 * 1024

HEAD_W = LANES
HALF_ROPE = QK_ROPE // 2
V_ROWS = V_DIM + 16
Q_SCALE = math.log2(math.e) / math.sqrt(QK_NOPE + QK_ROPE)

BF16 = jnp.bfloat16
F32 = jnp.float32


def _rms(x, g):
    ms = jnp.mean(x * x, axis=-1, keepdims=True)
    return x * lax.rsqrt(ms + EPS) * g


def _sigmoid(x):
    return 0.5 * (jnp.tanh(0.5 * x) + 1.0)


def _sigmoid_of_twice(h):
    return 0.5 * jnp.tanh(h) + 0.5


def _dot(a, b):
    return jnp.dot(a, b, preferred_element_type=F32)


def _dot_t(a, b):
    return lax.dot_general(a, b, (((1,), (1,)), ((), ())),
                           preferred_element_type=F32)


def _rope(g, c, s_lo, s_hi):
    up = pltpu.roll(g, LANES - HALF_ROPE, 1)
    dn = pltpu.roll(g, HALF_ROPE, 1)
    return g * c + up * s_lo + dn * s_hi


def _inproj_kernel(x_ref, tab_ref, tabt_ref, gpre_ref, wsm_ref, wbig_ref, gq_ref,
                   wqt_ref, gkv_ref, wk_ref, wvt_ref,
                   qt_ref, k_ref, vt_ref, u_ref, sga_ref, sgb_ref):
    d = x_ref.shape[1]
    xb = _rms(x_ref[...], gpre_ref[...]).astype(BF16)

    small = _dot(xb, wsm_ref[...])
    a = _dot(xb, wbig_ref[:, 0:d])
    g = _dot(xb, wbig_ref[:, d:2 * d])
    u_ref[...] = (a * _sigmoid_of_twice(g)).astype(BF16)
    qn = _rms(small[:, :Q_RANK], gq_ref[...]).astype(BF16)
    kvn = _rms(small[:, Q_RANK:Q_RANK + KV_RANK], gkv_ref[...]).astype(BF16)
    kr = _rope(small[:, Q_RANK + KV_RANK:], tab_ref[0], tab_ref[1], tab_ref[2])

    kin = jnp.concatenate([kvn, kr.astype(BF16)], axis=1)
    k_ref[...] = _dot(kin, wk_ref[...]).astype(BF16)
    vt = _dot_t(wvt_ref[...], kvn).astype(BF16)
    for h in range(N_HEADS):
        vt_ref[h * V_ROWS:h * V_ROWS + V_DIM, :] = vt[h * V_DIM:(h + 1) * V_DIM]
        vt_ref[h * V_ROWS + V_DIM:(h + 1) * V_ROWS, :] = jnp.ones(
            (V_ROWS - V_DIM, vt.shape[1]), BF16)

    qt = _dot_t(wqt_ref[...], qn)
    cos_t, sin_t = tabt_ref[0], tabt_ref[1]
    scale = Q_SCALE
    for h in range(N_HEADS):
        r0 = h * HEAD_W
        lo = qt[r0 + QK_NOPE:r0 + QK_NOPE + HALF_ROPE]
        hi = qt[r0 + QK_NOPE + HALF_ROPE:r0 + QK_NOPE + QK_ROPE]
        qt_ref[r0:r0 + QK_NOPE, :] = (qt[r0:r0 + QK_NOPE] * scale).astype(BF16)
        qt_ref[r0 + QK_NOPE:r0 + QK_NOPE + QK_ROPE, :] = jnp.concatenate(
            [lo * cos_t - hi * sin_t, hi * cos_t + lo * sin_t], axis=0).astype(BF16)
        qt_ref[r0 + QK_NOPE + QK_ROPE:r0 + HEAD_W, :] = jnp.zeros(
            (HEAD_W - QK_NOPE - QK_ROPE, qt.shape[1]), BF16)

    sga_ref[...] = _sigmoid_of_twice(_dot(xb, wbig_ref[:, 2 * d:3 * d])).astype(BF16)
    sgb_ref[...] = _sigmoid_of_twice(_dot(xb, wbig_ref[:, 3 * d:4 * d])).astype(BF16)


def _const_spec(shape):
    nd = len(shape)
    return pl.BlockSpec(shape, lambda *_: (0,) * nd, pipeline_mode=pl.Buffered(1))


def _inproj(x2d, tabs, tabs_t, weights, tm, seq):
    rows, d = x2d.shape
    gpre, wsm, wbig, gq, wqt, gkv, wk, wvt = weights
    n_q = wqt.shape[0]
    n_v = N_HEADS * V_ROWS
    tps = seq // tm
    row_spec = lambda w: pl.BlockSpec((tm, w), lambda i: (i, 0))
    col_spec = lambda h: pl.BlockSpec((None, h, tm), lambda i: (i // tps, 0, i % tps))
    out_shapes = [
        jax.ShapeDtypeStruct((rows // seq, n_q, seq), BF16),
        jax.ShapeDtypeStruct((rows, n_q), BF16),
        jax.ShapeDtypeStruct((rows // seq, n_v, seq), BF16),
        jax.ShapeDtypeStruct((rows, d), BF16),
        jax.ShapeDtypeStruct((rows, d), BF16),
        jax.ShapeDtypeStruct((rows, d), BF16),
    ]
    return pl.pallas_call(
        _inproj_kernel,
        grid=(rows // tm,),
        in_specs=[
            row_spec(d),
            pl.BlockSpec((3, tm, LANES), lambda i: (0, i % tps, 0)),
            pl.BlockSpec((2, HALF_ROPE, tm), lambda i: (0, 0, i % tps)),
            _const_spec(gpre.shape), _const_spec(wsm.shape),
            _const_spec(wbig.shape), _const_spec(gq.shape),
            _const_spec(wqt.shape), _const_spec(gkv.shape),
            _const_spec(wk.shape), _const_spec(wvt.shape),
        ],
        out_specs=[col_spec(n_q), row_spec(n_q), col_spec(n_v),
                   row_spec(d), row_spec(d), row_spec(d)],
        out_shape=out_shapes,
        compiler_params=pltpu.CompilerParams(
            dimension_semantics=("parallel",), vmem_limit_bytes=VMEM_LIMIT),
        name="inproj",
    )(x2d, tabs, tabs_t, gpre, wsm, wbig, gq, wqt, gkv, wk, wvt)


ATTN_LOOKAHEAD = 6
ATTN_HEADS = 4


def _attn_kernel(qt_ref, k_ref, vt_ref, km_ref, vmt_ref, o_ref, *, tq):
    s_len = k_ref.shape[0]
    key_chunk = lax.broadcasted_iota(jnp.int32, (tq, tq), 0) // CHUNK
    qry_chunk = lax.broadcasted_iota(jnp.int32, (tq, tq), 1) // CHUNK
    diag_ok = key_chunk <= qry_chunk

    items = []
    for i in reversed(range(s_len // tq)):
        for h in range(ATTN_HEADS):
            items += [(i, h, c) for c in [None] + list(range(i + 1))]

    def scores(i, h, c):
        ksl = slice(h * HEAD_W, (h + 1) * HEAD_W)
        qt = qt_ref[ksl, i * tq:(i + 1) * tq]
        if c is None:
            return _dot(km_ref[:, ksl], qt)
        s = _dot(k_ref[c * tq:(c + 1) * tq, ksl], qt)
        return jnp.where(diag_ok, s, NEG_INF) if c == i else s

    def update(state, s, h, c):
        vsl = slice(h * V_ROWS, (h + 1) * V_ROWS)
        cm = jnp.max(s, axis=0, keepdims=True)
        m_new = cm if state is None else jnp.maximum(state[0], cm)
        p = jnp.exp2(s - m_new).astype(BF16)
        v = vmt_ref[vsl, :] if c is None else vt_ref[vsl, c * tq:(c + 1) * tq]
        pv = _dot(v, p)
        if state is None:
            return m_new, pv
        return m_new, jnp.exp2(state[0] - m_new) * state[1] + pv

    pending = [scores(*items[n]) for n in range(min(ATTN_LOOKAHEAD, len(items)))]
    states, outs = {}, {}
    for n, (i, h, c) in enumerate(items):
        if n + ATTN_LOOKAHEAD < len(items):
            pending.append(scores(*items[n + ATTN_LOOKAHEAD]))
        states[h] = update(states.get(h), pending.pop(0), h, c)
        if c == i:
            acc = states.pop(h)[1]
            outs[h] = acc[:V_DIM] * (1.0 / acc[V_DIM:V_DIM + 1])
            if h % 2 == 1:
                ot = jnp.concatenate([outs[h - 1], outs[h]], axis=0)
                o_ref[i * tq:(i + 1) * tq, (h - 1) * V_DIM:(h + 1) * V_DIM] = ot.T.astype(o_ref.dtype)


def _attention(qt, k, vt, km, vmt, tq):
    b, s, _ = k.shape
    g = ATTN_HEADS
    return pl.pallas_call(
        functools.partial(_attn_kernel, tq=tq),
        grid=(b, N_HEADS // g),
        in_specs=[
            pl.BlockSpec((None, g * HEAD_W, s), lambda bi, hp: (bi, hp, 0)),
            pl.BlockSpec((None, s, g * HEAD_W), lambda bi, hp: (bi, 0, hp)),
            pl.BlockSpec((None, g * V_ROWS, s), lambda bi, hp: (bi, hp, 0)),
            pl.BlockSpec((N_META, g * HEAD_W), lambda bi, hp: (0, hp)),
            pl.BlockSpec((g * V_ROWS, N_META), lambda bi, hp: (hp, 0)),
        ],
        out_specs=pl.BlockSpec((None, s, g * V_DIM), lambda bi, hp: (bi, 0, hp)),
        out_shape=jax.ShapeDtypeStruct((b, s, N_HEADS * V_DIM), BF16),
        compiler_params=pltpu.CompilerParams(
            dimension_semantics=("parallel", "parallel"),
            vmem_limit_bytes=VMEM_LIMIT),
        name="attention",
    )(qt, k, vt, km, vmt)


CONV_PAD = 32
CONV_RB = 64
FF_CHUNK = 256
FF_GROUP = 4
PRE_FFN_UNITS = 3
CONV_SPARE = 8


def _tail_kernel(x_ref, o_ref, sga_ref, sgb_ref, u_ref, halo_ref, mhalo_ref,
                 cw_ref, cb_ref, lng_ref, lnb_ref, wao_ref, wco_ref, wout_ref,
                 gpost_ref, gfpre_ref, wfin_ref, wfout_ref, gfpost_ref,
                 out_ref, buf_ref, mix_ref, act_ref, *, tiles_per_seq, n_tiles):
    s = pl.program_id(0)
    tm, d = u_ref.shape
    d_ff = wfout_ref.shape[0]

    @pl.when(s == 0)
    def _():
        mix_ref[...] = jnp.zeros(mix_ref.shape, mix_ref.dtype)

    conv_tile = jnp.minimum(s, n_tiles - 1)
    first_in_seq = conv_tile % tiles_per_seq == 0
    row0 = jnp.minimum(s, 0)

    hist = jnp.where(first_in_seq, mhalo_ref[...], halo_ref[...]).astype(F32)
    for cb in range(d // LANES):
        csl = slice(cb * LANES, (cb + 1) * LANES)
        buf_ref[cb, 0:CONV_PAD, :] = hist[:, csl]
        buf_ref[cb, CONV_PAD:CONV_PAD + tm, :] = u_ref[:, csl].astype(F32)

    base = CONV_PAD - (CONV_K - 1)
    pieces = [(cb, r0) for cb in range(d // LANES) for r0 in range(0, tm, CONV_RB)]
    n_pieces = len(pieces)
    n_chunks = d_ff // FF_CHUNK
    n_groups = -(-n_chunks // FF_GROUP)
    wide, narrow = d // FF_CHUNK, 1
    budget = PRE_FFN_UNITS + (2 * n_chunks - 1) * narrow + (n_groups - 1) * wide
    spent = [0]
    yc_c0, hb_c0 = 0, d
    gate_row = pl.multiple_of(row0, 16)

    def gated(c0):
        return mix_ref[pl.ds(gate_row, tm), c0:c0 + d]

    def conv_pieces(cost):
        lo = min(spent[0], budget) * n_pieces // budget
        spent[0] += cost
        hi = min(spent[0], budget) * n_pieces // budget
        for _ in range(hi - lo):
            cb, r0 = pieces.pop(0)
            csl = slice(cb * LANES, (cb + 1) * LANES)
            acc = jnp.broadcast_to(cb_ref[:, csl], (CONV_RB, LANES))
            for res in range(8):
                taps = list(range(res, CONV_K, 8))
                n_rows = CONV_RB + 8 * (len(taps) - 1)
                win = buf_ref[cb, pl.ds(row0 + (r0 + base + res), n_rows), :]
                for t, kk in enumerate(taps):
                    acc = acc + cw_ref[kk:kk + 1, csl] * win[8 * t:8 * t + CONV_RB]
            mix_ref[r0:r0 + CONV_RB, yc_c0 + cb * LANES:yc_c0 + (cb + 1) * LANES] = acc.astype(BF16)

    yc = mix_ref[:, yc_c0:yc_c0 + d].astype(F32)
    mu = jnp.mean(yc, axis=-1, keepdims=True)
    xc = yc - mu
    var = jnp.mean(xc * xc, axis=-1, keepdims=True)
    z = xc * lax.rsqrt(var + EPS) * lng_ref[...] + lnb_ref[...]
    z = (z * _sigmoid(z)).astype(BF16)

    y_a = _dot(o_ref[...], wao_ref[...])
    y_b = _dot(z, wco_ref[...])
    merged = sga_ref[...].astype(F32) * y_a + sgb_ref[...].astype(F32) * y_b
    mo = _dot(merged.astype(BF16), wout_ref[...])
    buf_ref[0, CONV_PAD + tm:, :] = y_b[-CONV_SPARE:, -LANES:]
    conv_pieces(PRE_FFN_UNITS)
    x1 = x_ref[...] + _rms(mo, gpost_ref[...])

    mix_ref[:, hb_c0:hb_c0 + d] = _rms(x1, gfpre_ref[...]).astype(BF16)
    f = None
    for grp in range(n_groups):
        g0 = grp * FF_GROUP * FF_CHUNK
        g1 = min(g0 + FF_GROUP * FF_CHUNK, d_ff)
        for c0 in range(g0, g1, FF_CHUNK):
            g = _dot(gated(hb_c0), wfin_ref[:, c0:c0 + FF_CHUNK])
            conv_pieces(narrow)
            up = _dot(gated(hb_c0), wfin_ref[:, d_ff + c0:d_ff + c0 + FF_CHUNK])
            conv_pieces(narrow)
            act_ref[:, c0:c0 + FF_CHUNK] = ((g + g * jnp.tanh(g)) * up).astype(BF16)
        part = _dot(act_ref[:, g0:g1], wfout_ref[g0:g1, :])
        conv_pieces(wide)
        f = part if f is None else f + part
    assert not pieces
    out_ref[...] = x1 + _rms(f, gfpost_ref[...])


def _tail(x2d, o2d, sga, sgb, u2d, mhalo, weights, tm, seq):
    rows, d = x2d.shape
    d_ff = weights[10].shape[0]
    n_tiles = rows // tm
    halo_per_tile = tm // CONV_PAD
    lag_spec = pl.BlockSpec((tm, d), lambda s: (jnp.maximum(s - 1, 0), 0))
    conv_tile = lambda s: jnp.minimum(s, n_tiles - 1)
    return pl.pallas_call(
        functools.partial(_tail_kernel, tiles_per_seq=seq // tm, n_tiles=n_tiles),
        grid=(n_tiles + 1,),
        in_specs=[lag_spec] * 4 + [
            pl.BlockSpec((tm, d), lambda s: (conv_tile(s), 0)),
            pl.BlockSpec((CONV_PAD, d),
                         lambda s: (jnp.maximum(conv_tile(s) * halo_per_tile - 1, 0), 0)),
        ] + [_const_spec(mhalo.shape)] + [_const_spec(w.shape) for w in weights],
        out_specs=lag_spec,
        out_shape=jax.ShapeDtypeStruct((rows, d), F32),
        scratch_shapes=[pltpu.VMEM((d // LANES, tm + CONV_PAD + CONV_SPARE, LANES), F32),
                        pltpu.VMEM((tm, 2 * d), BF16),
                        pltpu.VMEM((tm, d_ff), BF16)],
        compiler_params=pltpu.CompilerParams(
            dimension_semantics=("arbitrary",), vmem_limit_bytes=VMEM_LIMIT),
        name="tail",
    )(x2d, o2d, sga, sgb, u2d, u2d, mhalo, *weights)


def _rope_tables(length):
    pos = np.arange(length, dtype=np.float32)
    inv = np.float32(ROPE_BASE) ** (-np.arange(0, QK_ROPE, 2, dtype=np.float32) / QK_ROPE)
    ang = (pos[:, None] * inv[None, :]).astype(np.float32)
    cos, sin = np.cos(ang), np.sin(ang)
    zeros = lambda w: np.zeros((length, w), np.float32)
    ktail = LANES - QK_ROPE
    ck = np.concatenate([cos, cos, zeros(ktail)], axis=1)
    sk_lo = np.concatenate([-sin, zeros(HALF_ROPE + ktail)], axis=1)
    sk_hi = np.concatenate([zeros(HALF_ROPE), sin, zeros(ktail)], axis=1)
    tabs = np.stack([ck, sk_lo, sk_hi]).astype(np.float32)
    tabs_t = (np.stack([cos.T, sin.T]) * np.float32(Q_SCALE)).astype(np.float32)
    return tabs, tabs_t


def kernel(x, meta, mix_pre_g, w_in, q_norm_g, w_uq, kv_norm_g, w_ukv, w_attn_o,
           conv_w, conv_b, conv_ln_g, conv_ln_b, w_conv_o, w_out, mix_post_g,
           ffn_pre_g, w_ffn_in, w_ffn_out, ffn_post_g):
    assert w_in.shape[0] == 1, "single-layer block"
    b, s, d = x.shape
    row = lambda g: g[0][None, :].astype(F32)

    w_in0 = w_in[0]
    off_glu = Q_RANK + KV_RANK + QK_ROPE
    wsm = jnp.pad(w_in0[:, :off_glu], ((0, 0), (0, 4 * LANES - off_glu))).astype(BF16)
    gate_scale = jnp.concatenate([jnp.ones((d,), F32), jnp.full((3 * d,), 0.5, F32)])
    wbig = (w_in0[:, off_glu:] * gate_scale).astype(BF16)

    qk_dim = QK_NOPE + QK_ROPE
    wq = w_uq[0].reshape(Q_RANK, N_HEADS, qk_dim)
    wq = jnp.pad(wq, ((0, 0), (0, 0), (0, HEAD_W - qk_dim)))
    wqt = wq.reshape(Q_RANK, N_HEADS * HEAD_W).T.astype(BF16)

    wkv = w_ukv[0].reshape(KV_RANK, N_HEADS, QK_NOPE + V_DIM)
    wk_nope = jnp.pad(wkv[:, :, :QK_NOPE], ((0, 0), (0, 0), (0, HEAD_W - QK_NOPE)))
    place = np.zeros((LANES, N_HEADS, HEAD_W), np.float32)
    jj = np.arange(QK_ROPE)
    place[jj, :, QK_NOPE + jj] = 1.0
    wk = jnp.concatenate([wk_nope, jnp.asarray(place)], axis=0)
    wk = wk.reshape(KV_RANK + LANES, N_HEADS * HEAD_W).astype(BF16)
    wvt = wkv[:, :, QK_NOPE:].reshape(KV_RANK, N_HEADS * V_DIM).T.astype(BF16)

    in_weights = (row(mix_pre_g), wsm, wbig, row(q_norm_g), wqt, row(kv_norm_g), wk, wvt)

    tabs, tabs_t = _rope_tables(N_META + s)

    tm = 256
    x2d = x.reshape(b * s, d)
    qt, k, vt, u, sga, sgb = _inproj(x2d, tabs[:, N_META:], tabs_t[:, :, N_META:],
                                     in_weights, 2 * tm, s)
    _, km, vmt, um, _, _ = _inproj(meta.astype(F32), tabs[:, :N_META],
                                   tabs_t[:, :, :N_META], in_weights, N_META, N_META)

    o = _attention(qt, k.reshape(b, s, -1), vt, km, vmt[0], tq=256)

    mhalo = jnp.concatenate([jnp.zeros((CONV_PAD - N_META, d), BF16), um], axis=0)
    d_ff = w_ffn_out.shape[1]
    ffn_scale = jnp.concatenate([jnp.full((d_ff,), 0.5, F32), jnp.ones((d_ff,), F32)])
    tail_weights = (conv_w[0].astype(F32), row(conv_b), row(conv_ln_g), row(conv_ln_b),
                    w_attn_o[0].astype(BF16), w_conv_o[0].astype(BF16),
                    w_out[0].astype(BF16), row(mix_post_g), row(ffn_pre_g),
                    (w_ffn_in[0] * ffn_scale).astype(BF16), w_ffn_out[0].astype(BF16),
                    row(ffn_post_g))
    out = _tail(x2d, o.reshape(b * s, -1), sga, sgb, u, mhalo, tail_weights, tm, s)
    return out.reshape(b, s, d)
```

```python
import functools
import math

import jax
import jax.numpy as jnp
import numpy as np
from jax import lax
from jax.experimental import pallas as pl
from jax.experimental.pallas import tpu as pltpu

CHUNK = 64
N_META = 16
N_HEADS = 16
QK_NOPE = 64
QK_ROPE = 32
V_DIM = 64
Q_RANK = 256
KV_RANK = 128
ROPE_BASE = 10000.0
CONV_K = 31
EPS = 1e-6
NEG_INF = -1e30

LANES = 128
VMEM_LIMIT = 56 * 1024 * 1024

HEAD_W = LANES
HALF_ROPE = QK_ROPE // 2
V_ROWS = V_DIM + 16
Q_SCALE = math.log2(math.e) / math.sqrt(QK_NOPE + QK_ROPE)

BF16 = jnp.bfloat16
F32 = jnp.float32


def _rms(x, g):
    ms = jnp.mean(x * x, axis=-1, keepdims=True)
    return x * lax.rsqrt(ms + EPS) * g


def _sigmoid(x):
    return 0.5 * (jnp.tanh(0.5 * x) + 1.0)


def _sigmoid_of_twice(h):
    return 0.5 * jnp.tanh(h) + 0.5


def _dot(a, b):
    return jnp.dot(a, b, preferred_element_type=F32)


def _dot_t(a, b):
    return lax.dot_general(a, b, (((1,), (1,)), ((), ())),
                           preferred_element_type=F32)


def _rope(g, c, s_lo, s_hi):
    up = pltpu.roll(g, LANES - HALF_ROPE, 1)
    dn = pltpu.roll(g, HALF_ROPE, 1)
    return g * c + up * s_lo + dn * s_hi


def _inproj_kernel(x_ref, tab_ref, tabt_ref, gpre_ref, wsm_ref, wbig_ref, gq_ref,
                   wqt_ref, gkv_ref, wk_ref, wvt_ref,
                   qt_ref, k_ref, vt_ref, u_ref, sga_ref, sgb_ref):
    d = x_ref.shape[1]
    xb = _rms(x_ref[...], gpre_ref[...]).astype(BF16)

    small = _dot(xb, wsm_ref[...])
    a = _dot(xb, wbig_ref[:, 0:d])
    g = _dot(xb, wbig_ref[:, d:2 * d])
    u_ref[...] = (a * _sigmoid_of_twice(g)).astype(BF16)
    qn = _rms(small[:, :Q_RANK], gq_ref[...]).astype(BF16)
    kvn = _rms(small[:, Q_RANK:Q_RANK + KV_RANK], gkv_ref[...]).astype(BF16)
    kr = _rope(small[:, Q_RANK + KV_RANK:], tab_ref[0], tab_ref[1], tab_ref[2])

    kin = jnp.concatenate([kvn, kr.astype(BF16)], axis=1)
    k_ref[...] = _dot(kin, wk_ref[...]).astype(BF16)
    vt = _dot_t(wvt_ref[...], kvn).astype(BF16)
    for h in range(N_HEADS):
        vt_ref[h * V_ROWS:h * V_ROWS + V_DIM, :] = vt[h * V_DIM:(h + 1) * V_DIM]
        vt_ref[h * V_ROWS + V_DIM:(h + 1) * V_ROWS, :] = jnp.ones(
            (V_ROWS - V_DIM, vt.shape[1]), BF16)

    qt = _dot_t(wqt_ref[...], qn)
    cos_t, sin_t = tabt_ref[0], tabt_ref[1]
    scale = Q_SCALE
    for h in range(N_HEADS):
        r0 = h * HEAD_W
        lo = qt[r0 + QK_NOPE:r0 + QK_NOPE + HALF_ROPE]
        hi = qt[r0 + QK_NOPE + HALF_ROPE:r0 + QK_NOPE + QK_ROPE]
        qt_ref[r0:r0 + QK_NOPE, :] = (qt[r0:r0 + QK_NOPE] * scale).astype(BF16)
        qt_ref[r0 + QK_NOPE:r0 + QK_NOPE + QK_ROPE, :] = jnp.concatenate(
            [lo * cos_t - hi * sin_t, hi * cos_t + lo * sin_t], axis=0).astype(BF16)
        qt_ref[r0 + QK_NOPE + QK_ROPE:r0 + HEAD_W, :] = jnp.zeros(
            (HEAD_W - QK_NOPE - QK_ROPE, qt.shape[1]), BF16)

    sga_ref[...] = _sigmoid_of_twice(_dot(xb, wbig_ref[:, 2 * d:3 * d])).astype(BF16)
    sgb_ref[...] = _sigmoid_of_twice(_dot(xb, wbig_ref[:, 3 * d:4 * d])).astype(BF16)


def _const_spec(shape):
    nd = len(shape)
    return pl.BlockSpec(shape, lambda *_: (0,) * nd, pipeline_mode=pl.Buffered(1))


def _inproj(x2d, tabs, tabs_t, weights, tm, seq):
    rows, d = x2d.shape
    gpre, wsm, wbig, gq, wqt, gkv, wk, wvt = weights
    n_q = wqt.shape[0]
    n_v = N_HEADS * V_ROWS
    tps = seq // tm
    row_spec = lambda w: pl.BlockSpec((tm, w), lambda i: (i, 0))
    col_spec = lambda h: pl.BlockSpec((None, h, tm), lambda i: (i // tps, 0, i % tps))
    out_shapes = [
        jax.ShapeDtypeStruct((rows // seq, n_q, seq), BF16),
        jax.ShapeDtypeStruct((rows, n_q), BF16),
        jax.ShapeDtypeStruct((rows // seq, n_v, seq), BF16),
        jax.ShapeDtypeStruct((rows, d), BF16),
        jax.ShapeDtypeStruct((rows, d), BF16),
        jax.ShapeDtypeStruct((rows, d), BF16),
    ]
    return pl.pallas_call(
        _inproj_kernel,
        grid=(rows // tm,),
        in_specs=[
            row_spec(d),
            pl.BlockSpec((3, tm, LANES), lambda i: (0, i % tps, 0)),
            pl.BlockSpec((2, HALF_ROPE, tm), lambda i: (0, 0, i % tps)),
            _const_spec(gpre.shape), _const_spec(wsm.shape),
            _const_spec(wbig.shape), _const_spec(gq.shape),
            _const_spec(wqt.shape), _const_spec(gkv.shape),
            _const_spec(wk.shape), _const_spec(wvt.shape),
        ],
        out_specs=[col_spec(n_q), row_spec(n_q), col_spec(n_v),
                   row_spec(d), row_spec(d), row_spec(d)],
        out_shape=out_shapes,
        compiler_params=pltpu.CompilerParams(
            dimension_semantics=("parallel",), vmem_limit_bytes=VMEM_LIMIT),
        name="inproj",
    )(x2d, tabs, tabs_t, gpre, wsm, wbig, gq, wqt, gkv, wk, wvt)


ATTN_LOOKAHEAD = 6
ATTN_HEADS = 4


def _attn_kernel(qt_ref, k_ref, vt_ref, km_ref, vmt_ref, o_ref, *, tq):
    s_len = k_ref.shape[0]
    key_chunk = lax.broadcasted_iota(jnp.int32, (tq, tq), 0) // CHUNK
    qry_chunk = lax.broadcasted_iota(jnp.int32, (tq, tq), 1) // CHUNK
    diag_ok = key_chunk <= qry_chunk

    items = []
    for i in reversed(range(s_len // tq)):
        for h in range(ATTN_HEADS):
            items += [(i, h, c) for c in [None] + list(range(i + 1))]

    def scores(i, h, c):
        ksl = slice(h * HEAD_W, (h + 1) * HEAD_W)
        qt = qt_ref[ksl, i * tq:(i + 1) * tq]
        if c is None:
            return _dot(km_ref[:, ksl], qt)
        s = _dot(k_ref[c * tq:(c + 1) * tq, ksl], qt)
        return jnp.where(diag_ok, s, NEG_INF) if c == i else s

    def update(state, s, h, c):
        vsl = slice(h * V_ROWS, (h + 1) * V_ROWS)
        cm = jnp.max(s, axis=0, keepdims=True)
        m_new = cm if state is None else jnp.maximum(state[0], cm)
        p = jnp.exp2(s - m_new).astype(BF16)
        v = vmt_ref[vsl, :] if c is None else vt_ref[vsl, c * tq:(c + 1) * tq]
        pv = _dot(v, p)
        if state is None:
            return m_new, pv
        return m_new, jnp.exp2(state[0] - m_new) * state[1] + pv

    pending = [scores(*items[n]) for n in range(min(ATTN_LOOKAHEAD, len(items)))]
    states, outs = {}, {}
    for n, (i, h, c) in enumerate(items):
        if n + ATTN_LOOKAHEAD < len(items):
            pending.append(scores(*items[n + ATTN_LOOKAHEAD]))
        states[h] = update(states.get(h), pending.pop(0), h, c)
        if c == i:
            acc = states.pop(h)[1]
            outs[h] = acc[:V_DIM] * (1.0 / acc[V_DIM:V_DIM + 1])
            if h % 2 == 1:
                ot = jnp.concatenate([outs[h - 1], outs[h]], axis=0)
                o_ref[i * tq:(i + 1) * tq, (h - 1) * V_DIM:(h + 1) * V_DIM] = ot.T.astype(o_ref.dtype)


def _attention(qt, k, vt, km, vmt, tq):
    b, s, _ = k.shape
    g = ATTN_HEADS
    return pl.pallas_call(
        functools.partial(_attn_kernel, tq=tq),
        grid=(b, N_HEADS // g),
        in_specs=[
            pl.BlockSpec((None, g * HEAD_W, s), lambda bi, hp: (bi, hp, 0)),
            pl.BlockSpec((None, s, g * HEAD_W), lambda bi, hp: (bi, 0, hp)),
            pl.BlockSpec((None, g * V_ROWS, s), lambda bi, hp: (bi, hp, 0)),
            pl.BlockSpec((N_META, g * HEAD_W), lambda bi, hp: (0, hp)),
            pl.BlockSpec((g * V_ROWS, N_META), lambda bi, hp: (hp, 0)),
        ],
        out_specs=pl.BlockSpec((None, s, g * V_DIM), lambda bi, hp: (bi, 0, hp)),
        out_shape=jax.ShapeDtypeStruct((b, s, N_HEADS * V_DIM), BF16),
        compiler_params=pltpu.CompilerParams(
            dimension_semantics=("parallel", "parallel"),
            vmem_limit_bytes=VMEM_LIMIT),
        name="attention",
    )(qt, k, vt, km, vmt)


CONV_PAD = 32
CONV_RB = 64
FF_CHUNK = 256
FF_GROUP = 4
SUB_ROWS = 256


def _tail_kernel(x_ref, o_ref, sga_ref, sgb_ref, u_ref, halo_ref, mhalo_ref,
                 cw_ref, cb_ref, lng_ref, lnb_ref, wao_ref, wco_ref, wout_ref,
                 gpost_ref, gfpre_ref, wfin_ref, wfout_ref, gfpost_ref,
                 out_ref, buf_ref, mix_ref, act_ref, *, tiles_per_seq, n_tiles):
    s = pl.program_id(0)
    tm, d = u_ref.shape
    d_ff = wfout_ref.shape[0]
    subs = [slice(r, r + SUB_ROWS) for r in range(0, tm, SUB_ROWS)]

    @pl.when(s == 0)
    def _():
        mix_ref[...] = jnp.zeros(mix_ref.shape, mix_ref.dtype)

    conv_tile = jnp.minimum(s, n_tiles - 1)
    first_in_seq = conv_tile % tiles_per_seq == 0
    row0 = jnp.minimum(s, 0)

    hist = jnp.where(first_in_seq, mhalo_ref[...], halo_ref[...]).astype(F32)
    for cb in range(d // LANES):
        csl = slice(cb * LANES, (cb + 1) * LANES)
        buf_ref[cb, 0:CONV_PAD, :] = hist[:, csl]
        buf_ref[cb, CONV_PAD:CONV_PAD + tm, :] = u_ref[:, csl].astype(F32)

    base = CONV_PAD - (CONV_K - 1)
    pieces = [(cb, r0) for cb in range(d // LANES) for r0 in range(0, tm, CONV_RB)]
    n_pieces = len(pieces)
    n_chunks = d_ff // FF_CHUNK
    n_groups = -(-n_chunks // FF_GROUP)
    wide, narrow = d // FF_CHUNK, 1
    budget = len(subs) * (2 * n_chunks * narrow + n_groups * wide) - narrow - wide
    spent = [0]
    yc_c0, hb_c0 = 0, d

    def gated(rs, c0):
        return mix_ref[pl.ds(pl.multiple_of(row0 + rs.start, 16), SUB_ROWS), c0:c0 + d]

    def conv_pieces(cost):
        lo = min(spent[0], budget) * n_pieces // budget
        spent[0] += cost
        hi = min(spent[0], budget) * n_pieces // budget
        for _ in range(hi - lo):
            cb, r0 = pieces.pop(0)
            csl = slice(cb * LANES, (cb + 1) * LANES)
            acc = jnp.broadcast_to(cb_ref[:, csl], (CONV_RB, LANES))
            for res in range(8):
                taps = list(range(res, CONV_K, 8))
                n_rows = CONV_RB + 8 * (len(taps) - 1)
                win = buf_ref[cb, pl.ds(row0 + (r0 + base + res), n_rows), :]
                for t, kk in enumerate(taps):
                    acc = acc + cw_ref[kk:kk + 1, csl] * win[8 * t:8 * t + CONV_RB]
            mix_ref[r0:r0 + CONV_RB, yc_c0 + cb * LANES:yc_c0 + (cb + 1) * LANES] = acc.astype(BF16)

    def front(rs):
        yc = mix_ref[rs, yc_c0:yc_c0 + d].astype(F32)
        mu = jnp.mean(yc, axis=-1, keepdims=True)
        xc = yc - mu
        var = jnp.mean(xc * xc, axis=-1, keepdims=True)
        z = xc * lax.rsqrt(var + EPS) * lng_ref[...] + lnb_ref[...]
        z = (z * _sigmoid(z)).astype(BF16)
        y_a = _dot(o_ref[rs, :], wao_ref[...])
        y_b = _dot(z, wco_ref[...])
        merged = sga_ref[rs, :].astype(F32) * y_a + sgb_ref[rs, :].astype(F32) * y_b
        mo = _dot(merged.astype(BF16), wout_ref[...])
        x1 = x_ref[rs, :] + _rms(mo, gpost_ref[...])
        return x1, _rms(x1, gfpre_ref[...]).astype(BF16)

    def ffn(rs, x1):
        f = None
        for grp in range(n_groups):
            g0 = grp * FF_GROUP * FF_CHUNK
            g1 = min(g0 + FF_GROUP * FF_CHUNK, d_ff)
            for c0 in range(g0, g1, FF_CHUNK):
                g = _dot(gated(rs, hb_c0), wfin_ref[:, c0:c0 + FF_CHUNK])
                conv_pieces(narrow)
                up = _dot(gated(rs, hb_c0), wfin_ref[:, d_ff + c0:d_ff + c0 + FF_CHUNK])
                conv_pieces(narrow)
                act_ref[rs, c0:c0 + FF_CHUNK] = ((g + g * jnp.tanh(g)) * up).astype(BF16)
            part = _dot(act_ref[rs, g0:g1], wfout_ref[g0:g1, :])
            conv_pieces(wide)
            f = part if f is None else f + part
        out_ref[rs, :] = x1 + _rms(f, gfpost_ref[...])

    fronts = [front(rs) for rs in subs]
    for rs, (x1, hb) in zip(subs, fronts):
        mix_ref[rs, hb_c0:hb_c0 + d] = hb
        ffn(rs, x1)
    assert not pieces


def _tail(x2d, o2d, sga, sgb, u2d, mhalo, weights, tm, seq):
    rows, d = x2d.shape
    d_ff = weights[10].shape[0]
    n_tiles = rows // tm
    halo_per_tile = tm // CONV_PAD
    lag_spec = pl.BlockSpec((tm, d), lambda s: (jnp.maximum(s - 1, 0), 0))
    conv_tile = lambda s: jnp.minimum(s, n_tiles - 1)
    return pl.pallas_call(
        functools.partial(_tail_kernel, tiles_per_seq=seq // tm, n_tiles=n_tiles),
        grid=(n_tiles + 1,),
        in_specs=[lag_spec] * 4 + [
            pl.BlockSpec((tm, d), lambda s: (conv_tile(s), 0)),
            pl.BlockSpec((CONV_PAD, d),
                         lambda s: (jnp.maximum(conv_tile(s) * halo_per_tile - 1, 0), 0)),
        ] + [_const_spec(mhalo.shape)] + [_const_spec(w.shape) for w in weights],
        out_specs=lag_spec,
        out_shape=jax.ShapeDtypeStruct((rows, d), F32),
        scratch_shapes=[pltpu.VMEM((d // LANES, tm + CONV_PAD, LANES), F32),
                        pltpu.VMEM((tm, 2 * d), BF16),
                        pltpu.VMEM((tm, d_ff), BF16)],
        compiler_params=pltpu.CompilerParams(
            dimension_semantics=("arbitrary",), vmem_limit_bytes=VMEM_LIMIT),
        name="tail",
    )(x2d, o2d, sga, sgb, u2d, u2d, mhalo, *weights)


def _rope_tables(length):
    pos = np.arange(length, dtype=np.float32)
    inv = np.float32(ROPE_BASE) ** (-np.arange(0, QK_ROPE, 2, dtype=np.float32) / QK_ROPE)
    ang = (pos[:, None] * inv[None, :]).astype(np.float32)
    cos, sin = np.cos(ang), np.sin(ang)
    zeros = lambda w: np.zeros((length, w), np.float32)
    ktail = LANES - QK_ROPE
    ck = np.concatenate([cos, cos, zeros(ktail)], axis=1)
    sk_lo = np.concatenate([-sin, zeros(HALF_ROPE + ktail)], axis=1)
    sk_hi = np.concatenate([zeros(HALF_ROPE), sin, zeros(ktail)], axis=1)
    tabs = np.stack([ck, sk_lo, sk_hi]).astype(np.float32)
    tabs_t = (np.stack([cos.T, sin.T]) * np.float32(Q_SCALE)).astype(np.float32)
    return tabs, tabs_t


def kernel(x, meta, mix_pre_g, w_in, q_norm_g, w_uq, kv_norm_g, w_ukv, w_attn_o,
           conv_w, conv_b, conv_ln_g, conv_ln_b, w_conv_o, w_out, mix_post_g,
           ffn_pre_g, w_ffn_in, w_ffn_out, ffn_post_g):
    assert w_in.shape[0] == 1, "single-layer block"
    b, s, d = x.shape
    row = lambda g: g[0][None, :].astype(F32)

    w_in0 = w_in[0]
    off_glu = Q_RANK + KV_RANK + QK_ROPE
    wsm = jnp.pad(w_in0[:, :off_glu], ((0, 0), (0, 4 * LANES - off_glu))).astype(BF16)
    gate_scale = jnp.concatenate([jnp.ones((d,), F32), jnp.full((3 * d,), 0.5, F32)])
    wbig = (w_in0[:, off_glu:] * gate_scale).astype(BF16)

    qk_dim = QK_NOPE + QK_ROPE
    wq = w_uq[0].reshape(Q_RANK, N_HEADS, qk_dim)
    wq = jnp.pad(wq, ((0, 0), (0, 0), (0, HEAD_W - qk_dim)))
    wqt = wq.reshape(Q_RANK, N_HEADS * HEAD_W).T.astype(BF16)

    wkv = w_ukv[0].reshape(KV_RANK, N_HEADS, QK_NOPE + V_DIM)
    wk_nope = jnp.pad(wkv[:, :, :QK_NOPE], ((0, 0), (0, 0), (0, HEAD_W - QK_NOPE)))
    place = np.zeros((LANES, N_HEADS, HEAD_W), np.float32)
    jj = np.arange(QK_ROPE)
    place[jj, :, QK_NOPE + jj] = 1.0
    wk = jnp.concatenate([wk_nope, jnp.asarray(place)], axis=0)
    wk = wk.reshape(KV_RANK + LANES, N_HEADS * HEAD_W).astype(BF16)
    wvt = wkv[:, :, QK_NOPE:].reshape(KV_RANK, N_HEADS * V_DIM).T.astype(BF16)

    in_weights = (row(mix_pre_g), wsm, wbig, row(q_norm_g), wqt, row(kv_norm_g), wk, wvt)

    tabs, tabs_t = _rope_tables(N_META + s)

    tm = 256
    x2d = x.reshape(b * s, d)
    qt, k, vt, u, sga, sgb = _inproj(x2d, tabs[:, N_META:], tabs_t[:, :, N_META:],
                                     in_weights, 2 * tm, s)
    _, km, vmt, um, _, _ = _inproj(meta.astype(F32), tabs[:, :N_META],
                                   tabs_t[:, :, :N_META], in_weights, N_META, N_META)

    o = _attention(qt, k.reshape(b, s, -1), vt, km, vmt[0], tq=256)

    mhalo = jnp.concatenate([jnp.zeros((CONV_PAD - N_META, d), BF16), um], axis=0)
    d_ff = w_ffn_out.shape[1]
    ffn_scale = jnp.concatenate([jnp.full((d_ff,), 0.5, F32), jnp.ones((d_ff,), F32)])
    tail_weights = (conv_w[0].astype(F32), row(conv_b), row(conv_ln_g), row(conv_ln_b),
                    w_attn_o[0].astype(BF16), w_conv_o[0].astype(BF16),
                    w_out[0].astype(BF16), row(mix_post_g), row(ffn_pre_g),
                    (w_ffn_in[0] * ffn_scale).astype(BF16), w_ffn_out[0].astype(BF16),
                    row(ffn_post_g))
    out = _tail(x2d, o.reshape(b * s, -1), sga, sgb, u, mhalo, tail_weights, 2 * tm, s)
    return out.reshape(b, s, d)
```

```python
import functools
import math

import jax
import jax.numpy as jnp
import numpy as np
from jax import lax
from jax.experimental import pallas as pl
from jax.experimental.pallas import tpu as pltpu

CHUNK = 64
N_META = 16
N_HEADS = 16
QK_NOPE = 64
QK_ROPE = 32
V_DIM = 64
Q_RANK = 256
KV_RANK = 128
ROPE_BASE = 10000.0
CONV_K = 31
EPS = 1e-6
NEG_INF = -1e30

LANES = 128
VMEM_LIMIT = 56 * 1024 * 1024

HEAD_W = LANES
HALF_ROPE = QK_ROPE // 2
V_ROWS = V_DIM + 16
Q_SCALE = math.log2(math.e) / math.sqrt(QK_NOPE + QK_ROPE)

BF16 = jnp.bfloat16
F32 = jnp.float32


def _rms(x, g):
    ms = jnp.mean(x * x, axis=-1, keepdims=True)
    return x * lax.rsqrt(ms + EPS) * g


def _sigmoid(x):
    return 0.5 * (jnp.tanh(0.5 * x) + 1.0)


def _sigmoid_of_twice(h):
    return 0.5 * jnp.tanh(h) + 0.5


def _dot(a, b):
    return jnp.dot(a, b, preferred_element_type=F32)


def _dot_t(a, b):
    return lax.dot_general(a, b, (((1,), (1,)), ((), ())),
                           preferred_element_type=F32)


def _rope(g, c, s_lo, s_hi):
    up = pltpu.roll(g, LANES - HALF_ROPE, 1)
    dn = pltpu.roll(g, HALF_ROPE, 1)
    return g * c + up * s_lo + dn * s_hi


def _inproj_kernel(x_ref, tab_ref, tabt_ref, gpre_ref, wsm_ref, wbig_ref, gq_ref,
                   wqt_ref, gkv_ref, wk_ref, wvt_ref,
                   qt_ref, k_ref, vt_ref, u_ref, sga_ref, sgb_ref):
    d = x_ref.shape[1]
    xb = _rms(x_ref[...], gpre_ref[...]).astype(BF16)

    small = _dot(xb, wsm_ref[...])
    a = _dot(xb, wbig_ref[:, 0:d])
    g = _dot(xb, wbig_ref[:, d:2 * d])
    u_ref[...] = (a * _sigmoid_of_twice(g)).astype(BF16)
    qn = _rms(small[:, :Q_RANK], gq_ref[...]).astype(BF16)
    kvn = _rms(small[:, Q_RANK:Q_RANK + KV_RANK], gkv_ref[...]).astype(BF16)
    kr = _rope(small[:, Q_RANK + KV_RANK:], tab_ref[0], tab_ref[1], tab_ref[2])

    kin = jnp.concatenate([kvn, kr.astype(BF16)], axis=1)
    k_ref[...] = _dot(kin, wk_ref[...]).astype(BF16)
    vt = _dot_t(wvt_ref[...], kvn).astype(BF16)
    for h in range(N_HEADS):
        vt_ref[h * V_ROWS:h * V_ROWS + V_DIM, :] = vt[h * V_DIM:(h + 1) * V_DIM]
        vt_ref[h * V_ROWS + V_DIM:(h + 1) * V_ROWS, :] = jnp.ones(
            (V_ROWS - V_DIM, vt.shape[1]), BF16)

    qt = _dot_t(wqt_ref[...], qn)
    cos_t, sin_t = tabt_ref[0], tabt_ref[1]
    scale = Q_SCALE
    for h in range(N_HEADS):
        r0 = h * HEAD_W
        lo = qt[r0 + QK_NOPE:r0 + QK_NOPE + HALF_ROPE]
        hi = qt[r0 + QK_NOPE + HALF_ROPE:r0 + QK_NOPE + QK_ROPE]
        qt_ref[r0:r0 + QK_NOPE, :] = (qt[r0:r0 + QK_NOPE] * scale).astype(BF16)
        qt_ref[r0 + QK_NOPE:r0 + QK_NOPE + QK_ROPE, :] = jnp.concatenate(
            [lo * cos_t - hi * sin_t, hi * cos_t + lo * sin_t], axis=0).astype(BF16)
        qt_ref[r0 + QK_NOPE + QK_ROPE:r0 + HEAD_W, :] = jnp.zeros(
            (HEAD_W - QK_NOPE - QK_ROPE, qt.shape[1]), BF16)

    sga_ref[...] = _sigmoid_of_twice(_dot(xb, wbig_ref[:, 2 * d:3 * d])).astype(BF16)
    sgb_ref[...] = _sigmoid_of_twice(_dot(xb, wbig_ref[:, 3 * d:4 * d])).astype(BF16)


def _const_spec(shape):
    nd = len(shape)
    return pl.BlockSpec(shape, lambda *_: (0,) * nd, pipeline_mode=pl.Buffered(1))


def _inproj(x2d, tabs, tabs_t, weights, tm, seq):
    rows, d = x2d.shape
    gpre, wsm, wbig, gq, wqt, gkv, wk, wvt = weights
    n_q = wqt.shape[0]
    n_v = N_HEADS * V_ROWS
    tps = seq // tm
    row_spec = lambda w: pl.BlockSpec((tm, w), lambda i: (i, 0))
    col_spec = lambda h: pl.BlockSpec((None, h, tm), lambda i: (i // tps, 0, i % tps))
    out_shapes = [
        jax.ShapeDtypeStruct((rows // seq, n_q, seq), BF16),
        jax.ShapeDtypeStruct((rows, n_q), BF16),
        jax.ShapeDtypeStruct((rows // seq, n_v, seq), BF16),
        jax.ShapeDtypeStruct((rows, d), BF16),
        jax.ShapeDtypeStruct((rows, d), BF16),
        jax.ShapeDtypeStruct((rows, d), BF16),
    ]
    return pl.pallas_call(
        _inproj_kernel,
        grid=(rows // tm,),
        in_specs=[
            row_spec(d),
            pl.BlockSpec((3, tm, LANES), lambda i: (0, i % tps, 0)),
            pl.BlockSpec((2, HALF_ROPE, tm), lambda i: (0, 0, i % tps)),
            _const_spec(gpre.shape), _const_spec(wsm.shape),
            _const_spec(wbig.shape), _const_spec(gq.shape),
            _const_spec(wqt.shape), _const_spec(gkv.shape),
            _const_spec(wk.shape), _const_spec(wvt.shape),
        ],
        out_specs=[col_spec(n_q), row_spec(n_q), col_spec(n_v),
                   row_spec(d), row_spec(d), row_spec(d)],
        out_shape=out_shapes,
        compiler_params=pltpu.CompilerParams(
            dimension_semantics=("parallel",), vmem_limit_bytes=VMEM_LIMIT),
        name="inproj",
    )(x2d, tabs, tabs_t, gpre, wsm, wbig, gq, wqt, gkv, wk, wvt)


ATTN_LOOKAHEAD = 6
ATTN_HEADS = 4


def _attn_kernel(qt_ref, k_ref, vt_ref, km_ref, vmt_ref, o_ref, *, tq):
    s_len = k_ref.shape[0]
    key_chunk = lax.broadcasted_iota(jnp.int32, (tq, tq), 0) // CHUNK
    qry_chunk = lax.broadcasted_iota(jnp.int32, (tq, tq), 1) // CHUNK
    diag_ok = key_chunk <= qry_chunk

    items = []
    for i in reversed(range(s_len // tq)):
        for h in range(ATTN_HEADS):
            items += [(i, h, c) for c in [None] + list(range(i + 1))]

    def scores(i, h, c):
        ksl = slice(h * HEAD_W, (h + 1) * HEAD_W)
        qt = qt_ref[ksl, i * tq:(i + 1) * tq]
        if c is None:
            return _dot(km_ref[:, ksl], qt)
        s = _dot(k_ref[c * tq:(c + 1) * tq, ksl], qt)
        return jnp.where(diag_ok, s, NEG_INF) if c == i else s

    def update(state, s, h, c):
        vsl = slice(h * V_ROWS, (h + 1) * V_ROWS)
        cm = jnp.max(s, axis=0, keepdims=True)
        m_new = cm if state is None else jnp.maximum(state[0], cm)
        p = jnp.exp2(s - m_new).astype(BF16)
        v = vmt_ref[vsl, :] if c is None else vt_ref[vsl, c * tq:(c + 1) * tq]
        pv = _dot(v, p)
        if state is None:
            return m_new, pv
        return m_new, jnp.exp2(state[0] - m_new) * state[1] + pv

    pending = [scores(*items[n]) for n in range(min(ATTN_LOOKAHEAD, len(items)))]
    states, outs = {}, {}
    for n, (i, h, c) in enumerate(items):
        if n + ATTN_LOOKAHEAD < len(items):
            pending.append(scores(*items[n + ATTN_LOOKAHEAD]))
        states[h] = update(states.get(h), pending.pop(0), h, c)
        if c == i:
            acc = states.pop(h)[1]
            outs[h] = acc[:V_DIM] * (1.0 / acc[V_DIM:V_DIM + 1])
            if h % 2 == 1:
                ot = jnp.concatenate([outs[h - 1], outs[h]], axis=0)
                o_ref[i * tq:(i + 1) * tq, (h - 1) * V_DIM:(h + 1) * V_DIM] = ot.T.astype(o_ref.dtype)


def _attention(qt, k, vt, km, vmt, tq):
    b, s, _ = k.shape
    g = ATTN_HEADS
    return pl.pallas_call(
        functools.partial(_attn_kernel, tq=tq),
        grid=(b, N_HEADS // g),
        in_specs=[
            pl.BlockSpec((None, g * HEAD_W, s), lambda bi, hp: (bi, hp, 0)),
            pl.BlockSpec((None, s, g * HEAD_W), lambda bi, hp: (bi, 0, hp)),
            pl.BlockSpec((None, g * V_ROWS, s), lambda bi, hp: (bi, hp, 0)),
            pl.BlockSpec((N_META, g * HEAD_W), lambda bi, hp: (0, hp)),
            pl.BlockSpec((g * V_ROWS, N_META), lambda bi, hp: (hp, 0)),
        ],
        out_specs=pl.BlockSpec((None, s, g * V_DIM), lambda bi, hp: (bi, 0, hp)),
        out_shape=jax.ShapeDtypeStruct((b, s, N_HEADS * V_DIM), BF16),
        compiler_params=pltpu.CompilerParams(
            dimension_semantics=("parallel", "parallel"),
            vmem_limit_bytes=VMEM_LIMIT),
        name="attention",
    )(qt, k, vt, km, vmt)


CONV_PAD = 32
CONV_RB = 64
FF_CHUNK = 256
FF_GROUP = 6
PRE_FFN_UNITS = 3
CONV_SPARE = 8


def _tail_kernel(x_ref, o_ref, sga_ref, sgb_ref, u_ref, halo_ref, mhalo_ref,
                 cw_ref, cb_ref, lng_ref, lnb_ref, wao_ref, wco_ref, wout_ref,
                 gpost_ref, gfpre_ref, wfin_ref, wfout_ref, gfpost_ref,
                 out_ref, buf_ref, mix_ref, act_ref, *, tiles_per_seq, n_tiles):
    s = pl.program_id(0)
    tm, d = u_ref.shape
    d_ff = wfout_ref.shape[0]

    @pl.when(s == 0)
    def _():
        mix_ref[...] = jnp.zeros(mix_ref.shape, mix_ref.dtype)

    conv_tile = jnp.minimum(s, n_tiles - 1)
    first_in_seq = conv_tile % tiles_per_seq == 0
    row0 = jnp.minimum(s, 0)

    hist = jnp.where(first_in_seq, mhalo_ref[...], halo_ref[...]).astype(F32)
    for cb in range(d // LANES):
        csl = slice(cb * LANES, (cb + 1) * LANES)
        buf_ref[cb, 0:CONV_PAD, :] = hist[:, csl]
        buf_ref[cb, CONV_PAD:CONV_PAD + tm, :] = u_ref[:, csl].astype(F32)

    base = CONV_PAD - (CONV_K - 1)
    pieces = [(cb, r0) for cb in range(d // LANES) for r0 in range(0, tm, CONV_RB)]
    n_pieces = len(pieces)
    n_chunks = d_ff // FF_CHUNK
    n_groups = -(-n_chunks // FF_GROUP)
    wide, narrow = d // FF_CHUNK, 1
    budget = PRE_FFN_UNITS + 2 * (n_chunks - 1) * narrow + (n_groups - 1) * wide
    spent = [0]
    yc_c0, hb_c0 = 0, d
    gate_row = pl.multiple_of(row0, 16)

    def gated(c0):
        return mix_ref[pl.ds(gate_row, tm), c0:c0 + d]

    def conv_pieces(cost):
        lo = min(spent[0], budget) * n_pieces // budget
        spent[0] += cost
        hi = min(spent[0], budget) * n_pieces // budget
        for _ in range(hi - lo):
            cb, r0 = pieces.pop(0)
            csl = slice(cb * LANES, (cb + 1) * LANES)
            acc = jnp.broadcast_to(cb_ref[:, csl], (CONV_RB, LANES))
            for res in range(8):
                taps = list(range(res, CONV_K, 8))
                n_rows = CONV_RB + 8 * (len(taps) - 1)
                win = buf_ref[cb, pl.ds(row0 + (r0 + base + res), n_rows), :]
                for t, kk in enumerate(taps):
                    acc = acc + cw_ref[kk:kk + 1, csl] * win[8 * t:8 * t + CONV_RB]
            mix_ref[r0:r0 + CONV_RB, yc_c0 + cb * LANES:yc_c0 + (cb + 1) * LANES] = acc.astype(BF16)

    yc = mix_ref[:, yc_c0:yc_c0 + d].astype(F32)
    mu = jnp.mean(yc, axis=-1, keepdims=True)
    xc = yc - mu
    var = jnp.mean(xc * xc, axis=-1, keepdims=True)
    z = xc * lax.rsqrt(var + EPS) * lng_ref[...] + lnb_ref[...]
    z = (z * _sigmoid(z)).astype(BF16)

    y_a = _dot(o_ref[...], wao_ref[...])
    y_b = _dot(z, wco_ref[...])
    merged = sga_ref[...].astype(F32) * y_a + sgb_ref[...].astype(F32) * y_b
    mo = _dot(merged.astype(BF16), wout_ref[...])
    buf_ref[0, CONV_PAD + tm:, :] = y_b[-CONV_SPARE:, -LANES:]
    conv_pieces(PRE_FFN_UNITS)
    x1 = x_ref[...] + _rms(mo, gpost_ref[...])

    mix_ref[:, hb_c0:hb_c0 + d] = _rms(x1, gfpre_ref[...]).astype(BF16)
    f = None
    for grp in range(n_groups):
        g0 = grp * FF_GROUP * FF_CHUNK
        g1 = min(g0 + FF_GROUP * FF_CHUNK, d_ff)
        for c0 in range(g0, g1, FF_CHUNK):
            hb = gated(hb_c0)
            g = _dot(hb, wfin_ref[:, c0:c0 + FF_CHUNK])
            conv_pieces(narrow)
            up = _dot(hb, wfin_ref[:, d_ff + c0:d_ff + c0 + FF_CHUNK])
            conv_pieces(narrow)
            act_ref[:, c0:c0 + FF_CHUNK] = ((g + g * jnp.tanh(g)) * up).astype(BF16)
        part = _dot(act_ref[:, g0:g1], wfout_ref[g0:g1, :])
        conv_pieces(wide)
        f = part if f is None else f + part
    assert not pieces
    out_ref[...] = x1 + _rms(f, gfpost_ref[...])


def _tail(x2d, o2d, sga, sgb, u2d, mhalo, weights, tm, seq):
    rows, d = x2d.shape
    d_ff = weights[10].shape[0]
    n_tiles = rows // tm
    halo_per_tile = tm // CONV_PAD
    lag_spec = pl.BlockSpec((tm, d), lambda s: (jnp.maximum(s - 1, 0), 0))
    conv_tile = lambda s: jnp.minimum(s, n_tiles - 1)
    return pl.pallas_call(
        functools.partial(_tail_kernel, tiles_per_seq=seq // tm, n_tiles=n_tiles),
        grid=(n_tiles + 1,),
        in_specs=[lag_spec] * 4 + [
            pl.BlockSpec((tm, d), lambda s: (conv_tile(s), 0)),
            pl.BlockSpec((CONV_PAD, d),
                         lambda s: (jnp.maximum(conv_tile(s) * halo_per_tile - 1, 0), 0)),
        ] + [_const_spec(mhalo.shape)] + [_const_spec(w.shape) for w in weights],
        out_specs=lag_spec,
        out_shape=jax.ShapeDtypeStruct((rows, d), F32),
        scratch_shapes=[pltpu.VMEM((d // LANES, tm + CONV_PAD + CONV_SPARE, LANES), F32),
                        pltpu.VMEM((tm, 2 * d), BF16),
                        pltpu.VMEM((tm, d_ff), BF16)],
        compiler_params=pltpu.CompilerParams(
            dimension_semantics=("arbitrary",), vmem_limit_bytes=VMEM_LIMIT),
        name="tail",
    )(x2d, o2d, sga, sgb, u2d, u2d, mhalo, *weights)


def _rope_tables(length):
    pos = np.arange(length, dtype=np.float32)
    inv = np.float32(ROPE_BASE) ** (-np.arange(0, QK_ROPE, 2, dtype=np.float32) / QK_ROPE)
    ang = (pos[:, None] * inv[None, :]).astype(np.float32)
    cos, sin = np.cos(ang), np.sin(ang)
    zeros = lambda w: np.zeros((length, w), np.float32)
    ktail = LANES - QK_ROPE
    ck = np.concatenate([cos, cos, zeros(ktail)], axis=1)
    sk_lo = np.concatenate([-sin, zeros(HALF_ROPE + ktail)], axis=1)
    sk_hi = np.concatenate([zeros(HALF_ROPE), sin, zeros(ktail)], axis=1)
    tabs = np.stack([ck, sk_lo, sk_hi]).astype(np.float32)
    tabs_t = (np.stack([cos.T, sin.T]) * np.float32(Q_SCALE)).astype(np.float32)
    return tabs, tabs_t


def kernel(x, meta, mix_pre_g, w_in, q_norm_g, w_uq, kv_norm_g, w_ukv, w_attn_o,
           conv_w, conv_b, conv_ln_g, conv_ln_b, w_conv_o, w_out, mix_post_g,
           ffn_pre_g, w_ffn_in, w_ffn_out, ffn_post_g):
    assert w_in.shape[0] == 1, "single-layer block"
    b, s, d = x.shape
    row = lambda g: g[0][None, :].astype(F32)

    w_in0 = w_in[0]
    off_glu = Q_RANK + KV_RANK + QK_ROPE
    wsm = jnp.pad(w_in0[:, :off_glu], ((0, 0), (0, 4 * LANES - off_glu))).astype(BF16)
    gate_scale = jnp.concatenate([jnp.ones((d,), F32), jnp.full((3 * d,), 0.5, F32)])
    wbig = (w_in0[:, off_glu:] * gate_scale).astype(BF16)

    qk_dim = QK_NOPE + QK_ROPE
    wq = w_uq[0].reshape(Q_RANK, N_HEADS, qk_dim)
    wq = jnp.pad(wq, ((0, 0), (0, 0), (0, HEAD_W - qk_dim)))
    wqt = wq.reshape(Q_RANK, N_HEADS * HEAD_W).T.astype(BF16)

    wkv = w_ukv[0].reshape(KV_RANK, N_HEADS, QK_NOPE + V_DIM)
    wk_nope = jnp.pad(wkv[:, :, :QK_NOPE], ((0, 0), (0, 0), (0, HEAD_W - QK_NOPE)))
    place = np.zeros((LANES, N_HEADS, HEAD_W), np.float32)
    jj = np.arange(QK_ROPE)
    place[jj, :, QK_NOPE + jj] = 1.0
    wk = jnp.concatenate([wk_nope, jnp.asarray(place)], axis=0)
    wk = wk.reshape(KV_RANK + LANES, N_HEADS * HEAD_W).astype(BF16)
    wvt = wkv[:, :, QK_NOPE:].reshape(KV_RANK, N_HEADS * V_DIM).T.astype(BF16)

    in_weights = (row(mix_pre_g), wsm, wbig, row(q_norm_g), wqt, row(kv_norm_g), wk, wvt)

    tabs, tabs_t = _rope_tables(N_META + s)

    tm = 256
    x2d = x.reshape(b * s, d)
    qt, k, vt, u, sga, sgb = _inproj(x2d, tabs[:, N_META:], tabs_t[:, :, N_META:],
                                     in_weights, 2 * tm, s)
    _, km, vmt, um, _, _ = _inproj(meta.astype(F32), tabs[:, :N_META],
                                   tabs_t[:, :, :N_META], in_weights, N_META, N_META)

    o = _attention(qt, k.reshape(b, s, -1), vt, km, vmt[0], tq=256)

    mhalo = jnp.concatenate([jnp.zeros((CONV_PAD - N_META, d), BF16), um], axis=0)
    d_ff = w_ffn_out.shape[1]
    ffn_scale = jnp.concatenate([jnp.full((d_ff,), 0.5, F32), jnp.ones((d_ff,), F32)])
    tail_weights = (conv_w[0].astype(F32), row(conv_b), row(conv_ln_g), row(conv_ln_b),
                    w_attn_o[0].astype(BF16), w_conv_o[0].astype(BF16),
                    w_out[0].astype(BF16), row(mix_post_g), row(ffn_pre_g),
                    (w_ffn_in[0] * ffn_scale).astype(BF16), w_ffn_out[0].astype(BF16),
                    row(ffn_post_g))
    out = _tail(x2d, o.reshape(b * s, -1), sga, sgb, u, mhalo, tail_weights, tm, s)
    return out.reshape(b, s, d)
```

```python
import functools
import math

import jax
import jax.numpy as jnp
import numpy as np
from jax import lax
from jax.experimental import pallas as pl
from jax.experimental.pallas import tpu as pltpu

CHUNK = 64
N_META = 16
N_HEADS = 16
QK_NOPE = 64
QK_ROPE = 32
V_DIM = 64
Q_RANK = 256
KV_RANK = 128
ROPE_BASE = 10000.0
CONV_K = 31
EPS = 1e-6
NEG_INF = -1e30

LANES = 128
VMEM_LIMIT = 56 * 1024 * 1024

HEAD_W = LANES
HALF_ROPE = QK_ROPE // 2
V_ROWS = V_DIM + 16
Q_SCALE = math.log2(math.e) / math.sqrt(QK_NOPE + QK_ROPE)

BF16 = jnp.bfloat16
F32 = jnp.float32


def _rms(x, g):
    ms = jnp.mean(x * x, axis=-1, keepdims=True)
    return x * lax.rsqrt(ms + EPS) * g


def _sigmoid(x):
    return 0.5 * (jnp.tanh(0.5 * x) + 1.0)


def _sigmoid_of_twice(h):
    return 0.5 * jnp.tanh(h) + 0.5


def _dot(a, b):
    return jnp.dot(a, b, preferred_element_type=F32)


def _dot_t(a, b):
    return lax.dot_general(a, b, (((1,), (1,)), ((), ())),
                           preferred_element_type=F32)


def _rope(g, c, s_lo, s_hi):
    up = pltpu.roll(g, LANES - HALF_ROPE, 1)
    dn = pltpu.roll(g, HALF_ROPE, 1)
    return g * c + up * s_lo + dn * s_hi


def _inproj_kernel(x_ref, tab_ref, tabt_ref, gpre_ref, wsm_ref, wbig_ref, gq_ref,
                   wqt_ref, gkv_ref, wk_ref, wvt_ref,
                   qt_ref, k_ref, vt_ref, u_ref, sga_ref, sgb_ref):
    d = x_ref.shape[1]
    xb = _rms(x_ref[...], gpre_ref[...]).astype(BF16)

    small = _dot(xb, wsm_ref[...])
    a = _dot(xb, wbig_ref[:, 0:d])
    g = _dot(xb, wbig_ref[:, d:2 * d])
    u_ref[...] = (a * _sigmoid_of_twice(g)).astype(BF16)
    qn = _rms(small[:, :Q_RANK], gq_ref[...]).astype(BF16)
    kvn = _rms(small[:, Q_RANK:Q_RANK + KV_RANK], gkv_ref[...]).astype(BF16)
    kr = _rope(small[:, Q_RANK + KV_RANK:], tab_ref[0], tab_ref[1], tab_ref[2])

    kin = jnp.concatenate([kvn, kr.astype(BF16)], axis=1)
    k_ref[...] = _dot(kin, wk_ref[...]).astype(BF16)
    vt = _dot_t(wvt_ref[...], kvn).astype(BF16)
    for h in range(N_HEADS):
        vt_ref[h * V_ROWS:h * V_ROWS + V_DIM, :] = vt[h * V_DIM:(h + 1) * V_DIM]
        vt_ref[h * V_ROWS + V_DIM:(h + 1) * V_ROWS, :] = jnp.ones(
            (V_ROWS - V_DIM, vt.shape[1]), BF16)

    qt = _dot_t(wqt_ref[...], qn)
    cos_t, sin_t = tabt_ref[0], tabt_ref[1]
    scale = Q_SCALE
    for h in range(N_HEADS):
        r0 = h * HEAD_W
        lo = qt[r0 + QK_NOPE:r0 + QK_NOPE + HALF_ROPE]
        hi = qt[r0 + QK_NOPE + HALF_ROPE:r0 + QK_NOPE + QK_ROPE]
        qt_ref[r0:r0 + QK_NOPE, :] = (qt[r0:r0 + QK_NOPE] * scale).astype(BF16)
        qt_ref[r0 + QK_NOPE:r0 + QK_NOPE + QK_ROPE, :] = jnp.concatenate(
            [lo * cos_t - hi * sin_t, hi * cos_t + lo * sin_t], axis=0).astype(BF16)
        qt_ref[r0 + QK_NOPE + QK_ROPE:r0 + HEAD_W, :] = jnp.zeros(
            (HEAD_W - QK_NOPE - QK_ROPE, qt.shape[1]), BF16)

    sga_ref[...] = _sigmoid_of_twice(_dot(xb, wbig_ref[:, 2 * d:3 * d])).astype(BF16)
    sgb_ref[...] = _sigmoid_of_twice(_dot(xb, wbig_ref[:, 3 * d:4 * d])).astype(BF16)


def _const_spec(shape):
    nd = len(shape)
    return pl.BlockSpec(shape, lambda *_: (0,) * nd, pipeline_mode=pl.Buffered(1))


def _inproj(x2d, tabs, tabs_t, weights, tm, seq):
    rows, d = x2d.shape
    gpre, wsm, wbig, gq, wqt, gkv, wk, wvt = weights
    n_q = wqt.shape[0]
    n_v = N_HEADS * V_ROWS
    tps = seq // tm
    row_spec = lambda w: pl.BlockSpec((tm, w), lambda i: (i, 0))
    col_spec = lambda h: pl.BlockSpec((None, h, tm), lambda i: (i // tps, 0, i % tps))
    out_shapes = [
        jax.ShapeDtypeStruct((rows // seq, n_q, seq), BF16),
        jax.ShapeDtypeStruct((rows, n_q), BF16),
        jax.ShapeDtypeStruct((rows // seq, n_v, seq), BF16),
        jax.ShapeDtypeStruct((rows, d), BF16),
        jax.ShapeDtypeStruct((rows, d), BF16),
        jax.ShapeDtypeStruct((rows, d), BF16),
    ]
    return pl.pallas_call(
        _inproj_kernel,
        grid=(rows // tm,),
        in_specs=[
            row_spec(d),
            pl.BlockSpec((3, tm, LANES), lambda i: (0, i % tps, 0)),
            pl.BlockSpec((2, HALF_ROPE, tm), lambda i: (0, 0, i % tps)),
            _const_spec(gpre.shape), _const_spec(wsm.shape),
            _const_spec(wbig.shape), _const_spec(gq.shape),
            _const_spec(wqt.shape), _const_spec(gkv.shape),
            _const_spec(wk.shape), _const_spec(wvt.shape),
        ],
        out_specs=[col_spec(n_q), row_spec(n_q), col_spec(n_v),
                   row_spec(d), row_spec(d), row_spec(d)],
        out_shape=out_shapes,
        compiler_params=pltpu.CompilerParams(
            dimension_semantics=("parallel",), vmem_limit_bytes=VMEM_LIMIT),
        name="inproj",
    )(x2d, tabs, tabs_t, gpre, wsm, wbig, gq, wqt, gkv, wk, wvt)


ATTN_LOOKAHEAD = 6
ATTN_HEADS = 4


def _attn_kernel(qt_ref, k_ref, vt_ref, km_ref, vmt_ref, o_ref, *, tq):
    s_len = k_ref.shape[0]
    key_chunk = lax.broadcasted_iota(jnp.int32, (tq, tq), 0) // CHUNK
    qry_chunk = lax.broadcasted_iota(jnp.int32, (tq, tq), 1) // CHUNK
    diag_ok = key_chunk <= qry_chunk

    items = []
    for i in reversed(range(s_len // tq)):
        for h in range(ATTN_HEADS):
            items += [(i, h, c) for c in [None] + list(range(i + 1))]

    def scores(i, h, c):
        ksl = slice(h * HEAD_W, (h + 1) * HEAD_W)
        qt = qt_ref[ksl, i * tq:(i + 1) * tq]
        if c is None:
            return _dot(km_ref[:, ksl], qt)
        s = _dot(k_ref[c * tq:(c + 1) * tq, ksl], qt)
        return jnp.where(diag_ok, s, NEG_INF) if c == i else s

    def update(state, s, h, c):
        vsl = slice(h * V_ROWS, (h + 1) * V_ROWS)
        cm = jnp.max(s, axis=0, keepdims=True)
        m_new = cm if state is None else jnp.maximum(state[0], cm)
        p = jnp.exp2(s - m_new).astype(BF16)
        v = vmt_ref[vsl, :] if c is None else vt_ref[vsl, c * tq:(c + 1) * tq]
        pv = _dot(v, p)
        if state is None:
            return m_new, pv
        return m_new, jnp.exp2(state[0] - m_new) * state[1] + pv

    pending = [scores(*items[n]) for n in range(min(ATTN_LOOKAHEAD, len(items)))]
    states, outs = {}, {}
    for n, (i, h, c) in enumerate(items):
        if n + ATTN_LOOKAHEAD < len(items):
            pending.append(scores(*items[n + ATTN_LOOKAHEAD]))
        states[h] = update(states.get(h), pending.pop(0), h, c)
        if c == i:
            acc = states.pop(h)[1]
            outs[h] = acc[:V_DIM] * (1.0 / acc[V_DIM:V_DIM + 1])
            if h % 2 == 1:
                ot = jnp.concatenate([outs[h - 1], outs[h]], axis=0)
                o_ref[i * tq:(i + 1) * tq, (h - 1) * V_DIM:(h + 1) * V_DIM] = ot.T.astype(o_ref.dtype)


def _attention(qt, k, vt, km, vmt, tq):
    b, s, _ = k.shape
    g = ATTN_HEADS
    return pl.pallas_call(
        functools.partial(_attn_kernel, tq=tq),
        grid=(b, N_HEADS // g),
        in_specs=[
            pl.BlockSpec((None, g * HEAD_W, s), lambda bi, hp: (bi, hp, 0)),
            pl.BlockSpec((None, s, g * HEAD_W), lambda bi, hp: (bi, 0, hp)),
            pl.BlockSpec((None, g * V_ROWS, s), lambda bi, hp: (bi, hp, 0)),
            pl.BlockSpec((N_META, g * HEAD_W), lambda bi, hp: (0, hp)),
            pl.BlockSpec((g * V_ROWS, N_META), lambda bi, hp: (hp, 0)),
        ],
        out_specs=pl.BlockSpec((None, s, g * V_DIM), lambda bi, hp: (bi, 0, hp)),
        out_shape=jax.ShapeDtypeStruct((b, s, N_HEADS * V_DIM), BF16),
        compiler_params=pltpu.CompilerParams(
            dimension_semantics=("parallel", "parallel"),
            vmem_limit_bytes=VMEM_LIMIT),
        name="attention",
    )(qt, k, vt, km, vmt)


CONV_PAD = 32
CONV_RB = 64
FF_CHUNK = 256
FF_GROUP = 6
PRE_FFN_UNITS = 3
CONV_SPARE = 8


def _tail_kernel(x_ref, o_ref, sga_ref, sgb_ref, u_ref, halo_ref, mhalo_ref,
                 cw_ref, cb_ref, lng_ref, lnb_ref, wao_ref, wco_ref, wout_ref,
                 gpost_ref, gfpre_ref, wfin_ref, wfout_ref, gfpost_ref,
                 out_ref, buf_ref, mix_ref, act_ref, *, tiles_per_seq, n_tiles):
    s = pl.program_id(0)
    tm, d = u_ref.shape
    d_ff = wfout_ref.shape[0]

    @pl.when(s == 0)
    def _():
        mix_ref[...] = jnp.zeros(mix_ref.shape, mix_ref.dtype)

    conv_tile = jnp.minimum(s, n_tiles - 1)
    first_in_seq = conv_tile % tiles_per_seq == 0
    row0 = jnp.minimum(s, 0)

    hist = jnp.where(first_in_seq, mhalo_ref[...], halo_ref[...]).astype(F32)
    for cb in range(d // LANES):
        csl = slice(cb * LANES, (cb + 1) * LANES)
        buf_ref[cb, 0:CONV_PAD, :] = hist[:, csl]
        buf_ref[cb, CONV_PAD:CONV_PAD + tm, :] = u_ref[:, csl].astype(F32)

    base = CONV_PAD - (CONV_K - 1)
    pieces = [(cb, r0) for cb in range(d // LANES) for r0 in range(0, tm, CONV_RB)]
    n_pieces = len(pieces)
    n_chunks = d_ff // FF_CHUNK
    n_groups = -(-n_chunks // FF_GROUP)
    wide, narrow = d // FF_CHUNK, 1
    budget = PRE_FFN_UNITS + 2 * (n_chunks - 1) * narrow + (n_groups - 1) * wide
    spent = [0]
    yc_c0, hb_c0 = 0, d
    gate_row = pl.multiple_of(row0, 16)

    def gated(c0):
        return mix_ref[pl.ds(gate_row, tm), c0:c0 + d]

    def conv_pieces(cost):
        lo = min(spent[0], budget) * n_pieces // budget
        spent[0] += cost
        hi = min(spent[0], budget) * n_pieces // budget
        for _ in range(hi - lo):
            cb, r0 = pieces.pop(0)
            csl = slice(cb * LANES, (cb + 1) * LANES)
            acc = jnp.broadcast_to(cb_ref[:, csl], (CONV_RB, LANES))
            for res in range(8):
                taps = list(range(res, CONV_K, 8))
                n_rows = CONV_RB + 8 * (len(taps) - 1)
                win = buf_ref[cb, pl.ds(row0 + (r0 + base + res), n_rows), :]
                for t, kk in enumerate(taps):
                    acc = acc + cw_ref[kk:kk + 1, csl] * win[8 * t:8 * t + CONV_RB]
            mix_ref[r0:r0 + CONV_RB, yc_c0 + cb * LANES:yc_c0 + (cb + 1) * LANES] = acc.astype(BF16)

    yc = mix_ref[:, yc_c0:yc_c0 + d].astype(F32)
    mu = jnp.mean(yc, axis=-1, keepdims=True)
    xc = yc - mu
    var = jnp.mean(xc * xc, axis=-1, keepdims=True)
    z = xc * lax.rsqrt(var + EPS) * lng_ref[...] + lnb_ref[...]
    z = (z * _sigmoid(z)).astype(BF16)

    y_a = _dot(o_ref[...], wao_ref[...])
    y_b = _dot(z, wco_ref[...])
    merged = sga_ref[...].astype(F32) * y_a + sgb_ref[...].astype(F32) * y_b
    mo = _dot(merged.astype(BF16), wout_ref[...])
    buf_ref[0, CONV_PAD + tm:, :] = y_b[-CONV_SPARE:, -LANES:]
    conv_pieces(PRE_FFN_UNITS)
    x1 = x_ref[...] + _rms(mo, gpost_ref[...])

    mix_ref[:, hb_c0:hb_c0 + d] = _rms(x1, gfpre_ref[...]).astype(BF16)
    f = None
    for grp in range(n_groups):
        g0 = grp * FF_GROUP * FF_CHUNK
        g1 = min(g0 + FF_GROUP * FF_CHUNK, d_ff)
        for c0 in range(g0, g1, FF_CHUNK):
            if (c0 // FF_CHUNK) % 2 == 0 or c0 == d_ff - FF_CHUNK:
                hb = gated(hb_c0)
            g = _dot(hb, wfin_ref[:, c0:c0 + FF_CHUNK])
            conv_pieces(narrow)
            up = _dot(hb, wfin_ref[:, d_ff + c0:d_ff + c0 + FF_CHUNK])
            conv_pieces(narrow)
            act_ref[:, c0:c0 + FF_CHUNK] = ((g + g * jnp.tanh(g)) * up).astype(BF16)
        part = _dot(act_ref[:, g0:g1], wfout_ref[g0:g1, :])
        conv_pieces(wide)
        f = part if f is None else f + part
    assert not pieces
    out_ref[...] = x1 + _rms(f, gfpost_ref[...])


def _tail(x2d, o2d, sga, sgb, u2d, mhalo, weights, tm, seq):
    rows, d = x2d.shape
    d_ff = weights[10].shape[0]
    n_tiles = rows // tm
    halo_per_tile = tm // CONV_PAD
    lag_spec = pl.BlockSpec((tm, d), lambda s: (jnp.maximum(s - 1, 0), 0))
    conv_tile = lambda s: jnp.minimum(s, n_tiles - 1)
    return pl.pallas_call(
        functools.partial(_tail_kernel, tiles_per_seq=seq // tm, n_tiles=n_tiles),
        grid=(n_tiles + 1,),
        in_specs=[lag_spec] * 4 + [
            pl.BlockSpec((tm, d), lambda s: (conv_tile(s), 0)),
            pl.BlockSpec((CONV_PAD, d),
                         lambda s: (jnp.maximum(conv_tile(s) * halo_per_tile - 1, 0), 0)),
        ] + [_const_spec(mhalo.shape)] + [_const_spec(w.shape) for w in weights],
        out_specs=lag_spec,
        out_shape=jax.ShapeDtypeStruct((rows, d), F32),
        scratch_shapes=[pltpu.VMEM((d // LANES, tm + CONV_PAD + CONV_SPARE, LANES), F32),
                        pltpu.VMEM((tm, 2 * d), BF16),
                        pltpu.VMEM((tm, d_ff), BF16)],
        compiler_params=pltpu.CompilerParams(
            dimension_semantics=("arbitrary",), vmem_limit_bytes=VMEM_LIMIT),
        name="tail",
    )(x2d, o2d, sga, sgb, u2d, u2d, mhalo, *weights)


def _rope_tables(length):
    pos = np.arange(length, dtype=np.float32)
    inv = np.float32(ROPE_BASE) ** (-np.arange(0, QK_ROPE, 2, dtype=np.float32) / QK_ROPE)
    ang = (pos[:, None] * inv[None, :]).astype(np.float32)
    cos, sin = np.cos(ang), np.sin(ang)
    zeros = lambda w: np.zeros((length, w), np.float32)
    ktail = LANES - QK_ROPE
    ck = np.concatenate([cos, cos, zeros(ktail)], axis=1)
    sk_lo = np.concatenate([-sin, zeros(HALF_ROPE + ktail)], axis=1)
    sk_hi = np.concatenate([zeros(HALF_ROPE), sin, zeros(ktail)], axis=1)
    tabs = np.stack([ck, sk_lo, sk_hi]).astype(np.float32)
    tabs_t = (np.stack([cos.T, sin.T]) * np.float32(Q_SCALE)).astype(np.float32)
    return tabs, tabs_t


def kernel(x, meta, mix_pre_g, w_in, q_norm_g, w_uq, kv_norm_g, w_ukv, w_attn_o,
           conv_w, conv_b, conv_ln_g, conv_ln_b, w_conv_o, w_out, mix_post_g,
           ffn_pre_g, w_ffn_in, w_ffn_out, ffn_post_g):
    assert w_in.shape[0] == 1, "single-layer block"
    b, s, d = x.shape
    row = lambda g: g[0][None, :].astype(F32)

    w_in0 = w_in[0]
    off_glu = Q_RANK + KV_RANK + QK_ROPE
    wsm = w_in0[:, :4 * LANES].astype(BF16)
    gate_scale = jnp.concatenate([jnp.ones((d,), F32), jnp.full((3 * d,), 0.5, F32)])
    wbig = (w_in0[:, off_glu:] * gate_scale).astype(BF16)

    qk_dim = QK_NOPE + QK_ROPE
    wq = w_uq[0].reshape(Q_RANK, N_HEADS, qk_dim)
    wq = jnp.pad(wq, ((0, 0), (0, 0), (0, HEAD_W - qk_dim)))
    wqt = wq.reshape(Q_RANK, N_HEADS * HEAD_W).T.astype(BF16)

    wkv = w_ukv[0].reshape(KV_RANK, N_HEADS, QK_NOPE + V_DIM)
    wk_nope = jnp.pad(wkv[:, :, :QK_NOPE], ((0, 0), (0, 0), (0, HEAD_W - QK_NOPE)))
    place = np.zeros((LANES, N_HEADS, HEAD_W), np.float32)
    jj = np.arange(QK_ROPE)
    place[jj, :, QK_NOPE + jj] = 1.0
    wk = jnp.concatenate([wk_nope, jnp.asarray(place)], axis=0)
    wk = wk.reshape(KV_RANK + LANES, N_HEADS * HEAD_W).astype(BF16)
    wvt = wkv[:, :, QK_NOPE:].reshape(KV_RANK, N_HEADS * V_DIM).T.astype(BF16)

    in_weights = (row(mix_pre_g), wsm, wbig, row(q_norm_g), wqt, row(kv_norm_g), wk, wvt)

    tabs, tabs_t = _rope_tables(N_META + s)

    tm = 256
    x2d = x.reshape(b * s, d)
    qt, k, vt, u, sga, sgb = _inproj(x2d, tabs[:, N_META:], tabs_t[:, :, N_META:],
                                     in_weights, 2 * tm, s)
    _, km, vmt, um, _, _ = _inproj(meta.astype(F32), tabs[:, :N_META],
                                   tabs_t[:, :, :N_META], in_weights, N_META, N_META)

    o = _attention(qt, k.reshape(b, s, -1), vt, km, vmt[0], tq=256)

    mhalo = jnp.concatenate([jnp.zeros((CONV_PAD - N_META, d), BF16), um], axis=0)
    d_ff = w_ffn_out.shape[1]
    ffn_scale = jnp.concatenate([jnp.full((d_ff,), 0.5, F32), jnp.ones((d_ff,), F32)])
    tail_weights = (conv_w[0].astype(F32), row(conv_b), row(conv_ln_g), row(conv_ln_b),
                    w_attn_o[0].astype(BF16), w_conv_o[0].astype(BF16),
                    w_out[0].astype(BF16), row(mix_post_g), row(ffn_pre_g),
                    (w_ffn_in[0] * ffn_scale).astype(BF16), w_ffn_out[0].astype(BF16),
                    row(ffn_post_g))
    out = _tail(x2d, o.reshape(b * s, -1), sga, sgb, u, mhalo, tail_weights, tm, s)
    return out.reshape(b, s, d)
```

```python
import functools
import math

import jax
import jax.numpy as jnp
import numpy as np
from jax import lax
from jax.experimental import pallas as pl
from jax.experimental.pallas import tpu as pltpu

CHUNK = 64
N_META = 16
N_HEADS = 16
QK_NOPE = 64
QK_ROPE = 32
V_DIM = 64
Q_RANK = 256
KV_RANK = 128
ROPE_BASE = 10000.0
CONV_K = 31
EPS = 1e-6
NEG_INF = -1e30

LANES = 128
VMEM_LIMIT = 56 * 1024 * 1024

HEAD_W = LANES
HALF_ROPE = QK_ROPE // 2
V_ROWS = V_DIM + 16
Q_SCALE = math.log2(math.e) / math.sqrt(QK_NOPE + QK_ROPE)

BF16 = jnp.bfloat16
F32 = jnp.float32


def _rms(x, g):
    ms = jnp.mean(x * x, axis=-1, keepdims=True)
    return x * lax.rsqrt(ms + EPS) * g


def _sigmoid(x):
    return 0.5 * (jnp.tanh(0.5 * x) + 1.0)


def _sigmoid_of_twice(h):
    return 0.5 * jnp.tanh(h) + 0.5


def _dot(a, b):
    return jnp.dot(a, b, preferred_element_type=F32)


def _dot_t(a, b):
    return lax.dot_general(a, b, (((1,), (1,)), ((), ())),
                           preferred_element_type=F32)


def _rope(g, c, s_lo, s_hi):
    up = pltpu.roll(g, LANES - HALF_ROPE, 1)
    dn = pltpu.roll(g, HALF_ROPE, 1)
    return g * c + up * s_lo + dn * s_hi


def _inproj_kernel(x_ref, tab_ref, tabt_ref, gpre_ref, wsm_ref, wbig_ref, gq_ref,
                   wqt_ref, gkv_ref, wk_ref, wvt_ref,
                   qt_ref, k_ref, vt_ref, u_ref, sga_ref, sgb_ref):
    d = x_ref.shape[1]
    xb = _rms(x_ref[...], gpre_ref[...]).astype(BF16)

    small = _dot(xb, wsm_ref[...])
    a = _dot(xb, wbig_ref[:, 0:d])
    g = _dot(xb, wbig_ref[:, d:2 * d])
    u_ref[...] = (a * _sigmoid_of_twice(g)).astype(BF16)
    qn = _rms(small[:, :Q_RANK], gq_ref[...]).astype(BF16)
    kvn = _rms(small[:, Q_RANK:Q_RANK + KV_RANK], gkv_ref[...]).astype(BF16)
    kr = _rope(small[:, Q_RANK + KV_RANK:], tab_ref[0], tab_ref[1], tab_ref[2])

    kin = jnp.concatenate([kvn, kr.astype(BF16)], axis=1)
    k_ref[...] = _dot(kin, wk_ref[...]).astype(BF16)
    vt = _dot_t(wvt_ref[...], kvn).astype(BF16)
    for h in range(N_HEADS):
        vt_ref[h * V_ROWS:h * V_ROWS + V_DIM, :] = vt[h * V_DIM:(h + 1) * V_DIM]
        vt_ref[h * V_ROWS + V_DIM:(h + 1) * V_ROWS, :] = jnp.ones(
            (V_ROWS - V_DIM, vt.shape[1]), BF16)

    qt = _dot_t(wqt_ref[...], qn)
    cos_t, sin_t = tabt_ref[0], tabt_ref[1]
    scale = Q_SCALE
    for h in range(N_HEADS):
        r0 = h * HEAD_W
        lo = qt[r0 + QK_NOPE:r0 + QK_NOPE + HALF_ROPE]
        hi = qt[r0 + QK_NOPE + HALF_ROPE:r0 + QK_NOPE + QK_ROPE]
        qt_ref[r0:r0 + QK_NOPE, :] = (qt[r0:r0 + QK_NOPE] * scale).astype(BF16)
        qt_ref[r0 + QK_NOPE:r0 + QK_NOPE + QK_ROPE, :] = jnp.concatenate(
            [lo * cos_t - hi * sin_t, hi * cos_t + lo * sin_t], axis=0).astype(BF16)
        qt_ref[r0 + QK_NOPE + QK_ROPE:r0 + HEAD_W, :] = jnp.zeros(
            (HEAD_W - QK_NOPE - QK_ROPE, qt.shape[1]), BF16)

    sga_ref[...] = _sigmoid_of_twice(_dot(xb, wbig_ref[:, 2 * d:3 * d])).astype(BF16)
    sgb_ref[...] = _sigmoid_of_twice(_dot(xb, wbig_ref[:, 3 * d:4 * d])).astype(BF16)


def _inweight_kernel(w_ref, wsm_ref, wbig_ref):
    w = w_ref[...]
    n_small = wsm_ref.shape[1]
    d = wbig_ref.shape[1] // 4
    off = w.shape[1] - 4 * d
    wsm_ref[...] = w[:, :n_small].astype(BF16)
    wbig_ref[:, :d] = w[:, off:off + d].astype(BF16)
    wbig_ref[:, d:] = (0.5 * w[:, off + d:]).astype(BF16)


def _inweights(w_in0, n_small, rows_per_step=128):
    k, n = w_in0.shape
    n_big = n - (Q_RANK + KV_RANK + QK_ROPE)
    return pl.pallas_call(
        _inweight_kernel,
        grid=(k // rows_per_step,),
        in_specs=[pl.BlockSpec((rows_per_step, n), lambda i: (i, 0))],
        out_specs=[pl.BlockSpec((rows_per_step, n_small), lambda i: (i, 0)),
                   pl.BlockSpec((rows_per_step, n_big), lambda i: (i, 0))],
        out_shape=[jax.ShapeDtypeStruct((k, n_small), BF16),
                   jax.ShapeDtypeStruct((k, n_big), BF16)],
        compiler_params=pltpu.CompilerParams(
            dimension_semantics=("parallel",), vmem_limit_bytes=VMEM_LIMIT),
        name="inweights",
    )(w_in0)


def _const_spec(shape):
    nd = len(shape)
    return pl.BlockSpec(shape, lambda *_: (0,) * nd, pipeline_mode=pl.Buffered(1))


def _inproj(x2d, tabs, tabs_t, weights, tm, seq):
    rows, d = x2d.shape
    gpre, wsm, wbig, gq, wqt, gkv, wk, wvt = weights
    n_q = wqt.shape[0]
    n_v = N_HEADS * V_ROWS
    tps = seq // tm
    row_spec = lambda w: pl.BlockSpec((tm, w), lambda i: (i, 0))
    col_spec = lambda h: pl.BlockSpec((None, h, tm), lambda i: (i // tps, 0, i % tps))
    out_shapes = [
        jax.ShapeDtypeStruct((rows // seq, n_q, seq), BF16),
        jax.ShapeDtypeStruct((rows, n_q), BF16),
        jax.ShapeDtypeStruct((rows // seq, n_v, seq), BF16),
        jax.ShapeDtypeStruct((rows, d), BF16),
        jax.ShapeDtypeStruct((rows, d), BF16),
        jax.ShapeDtypeStruct((rows, d), BF16),
    ]
    return pl.pallas_call(
        _inproj_kernel,
        grid=(rows // tm,),
        in_specs=[
            row_spec(d),
            pl.BlockSpec((3, tm, LANES), lambda i: (0, i % tps, 0)),
            pl.BlockSpec((2, HALF_ROPE, tm), lambda i: (0, 0, i % tps)),
            _const_spec(gpre.shape), _const_spec(wsm.shape),
            _const_spec(wbig.shape), _const_spec(gq.shape),
            _const_spec(wqt.shape), _const_spec(gkv.shape),
            _const_spec(wk.shape), _const_spec(wvt.shape),
        ],
        out_specs=[col_spec(n_q), row_spec(n_q), col_spec(n_v),
                   row_spec(d), row_spec(d), row_spec(d)],
        out_shape=out_shapes,
        compiler_params=pltpu.CompilerParams(
            dimension_semantics=("parallel",), vmem_limit_bytes=VMEM_LIMIT),
        name="inproj",
    )(x2d, tabs, tabs_t, gpre, wsm, wbig, gq, wqt, gkv, wk, wvt)


ATTN_LOOKAHEAD = 6
ATTN_HEADS = 4


def _attn_kernel(qt_ref, k_ref, vt_ref, km_ref, vmt_ref, o_ref, *, tq):
    s_len = k_ref.shape[0]
    key_chunk = lax.broadcasted_iota(jnp.int32, (tq, tq), 0) // CHUNK
    qry_chunk = lax.broadcasted_iota(jnp.int32, (tq, tq), 1) // CHUNK
    diag_ok = key_chunk <= qry_chunk

    items = []
    for i in reversed(range(s_len // tq)):
        for h in range(ATTN_HEADS):
            items += [(i, h, c) for c in [None] + list(range(i + 1))]

    def scores(i, h, c):
        ksl = slice(h * HEAD_W, (h + 1) * HEAD_W)
        qt = qt_ref[ksl, i * tq:(i + 1) * tq]
        if c is None:
            return _dot(km_ref[:, ksl], qt)
        s = _dot(k_ref[c * tq:(c + 1) * tq, ksl], qt)
        return jnp.where(diag_ok, s, NEG_INF) if c == i else s

    def update(state, s, h, c):
        vsl = slice(h * V_ROWS, (h + 1) * V_ROWS)
        cm = jnp.max(s, axis=0, keepdims=True)
        m_new = cm if state is None else jnp.maximum(state[0], cm)
        p = jnp.exp2(s - m_new).astype(BF16)
        v = vmt_ref[vsl, :] if c is None else vt_ref[vsl, c * tq:(c + 1) * tq]
        pv = _dot(v, p)
        if state is None:
            return m_new, pv
        return m_new, jnp.exp2(state[0] - m_new) * state[1] + pv

    pending = [scores(*items[n]) for n in range(min(ATTN_LOOKAHEAD, len(items)))]
    states, outs = {}, {}
    for n, (i, h, c) in enumerate(items):
        if n + ATTN_LOOKAHEAD < len(items):
            pending.append(scores(*items[n + ATTN_LOOKAHEAD]))
        states[h] = update(states.get(h), pending.pop(0), h, c)
        if c == i:
            acc = states.pop(h)[1]
            outs[h] = acc[:V_DIM] * (1.0 / acc[V_DIM:V_DIM + 1])
            if h % 2 == 1:
                ot = jnp.concatenate([outs[h - 1], outs[h]], axis=0)
                o_ref[i * tq:(i + 1) * tq, (h - 1) * V_DIM:(h + 1) * V_DIM] = ot.T.astype(o_ref.dtype)


def _attention(qt, k, vt, km, vmt, tq):
    b, s, _ = k.shape
    g = ATTN_HEADS
    return pl.pallas_call(
        functools.partial(_attn_kernel, tq=tq),
        grid=(b, N_HEADS // g),
        in_specs=[
            pl.BlockSpec((None, g * HEAD_W, s), lambda bi, hp: (bi, hp, 0)),
            pl.BlockSpec((None, s, g * HEAD_W), lambda bi, hp: (bi, 0, hp)),
            pl.BlockSpec((None, g * V_ROWS, s), lambda bi, hp: (bi, hp, 0)),
            pl.BlockSpec((N_META, g * HEAD_W), lambda bi, hp: (0, hp)),
            pl.BlockSpec((g * V_ROWS, N_META), lambda bi, hp: (hp, 0)),
        ],
        out_specs=pl.BlockSpec((None, s, g * V_DIM), lambda bi, hp: (bi, 0, hp)),
        out_shape=jax.ShapeDtypeStruct((b, s, N_HEADS * V_DIM), BF16),
        compiler_params=pltpu.CompilerParams(
            dimension_semantics=("parallel", "parallel"),
            vmem_limit_bytes=VMEM_LIMIT),
        name="attention",
    )(qt, k, vt, km, vmt)


CONV_PAD = 32
CONV_RB = 64
FF_CHUNK = 256
FF_GROUP = 6
PRE_FFN_UNITS = 3
CONV_SPARE = 8


def _tail_kernel(x_ref, o_ref, sga_ref, sgb_ref, u_ref, halo_ref, mhalo_ref,
                 cw_ref, cb_ref, lng_ref, lnb_ref, wao_ref, wco_ref, wout_ref,
                 gpost_ref, gfpre_ref, wfin_ref, wfout_ref, gfpost_ref,
                 out_ref, buf_ref, mix_ref, act_ref, *, tiles_per_seq, n_tiles):
    s = pl.program_id(0)
    tm, d = u_ref.shape
    d_ff = wfout_ref.shape[0]

    @pl.when(s == 0)
    def _():
        mix_ref[...] = jnp.zeros(mix_ref.shape, mix_ref.dtype)

    conv_tile = jnp.minimum(s, n_tiles - 1)
    first_in_seq = conv_tile % tiles_per_seq == 0
    row0 = jnp.minimum(s, 0)

    hist = jnp.where(first_in_seq, mhalo_ref[...], halo_ref[...]).astype(F32)
    for cb in range(d // LANES):
        csl = slice(cb * LANES, (cb + 1) * LANES)
        buf_ref[cb, 0:CONV_PAD, :] = hist[:, csl]
        buf_ref[cb, CONV_PAD:CONV_PAD + tm, :] = u_ref[:, csl].astype(F32)

    base = CONV_PAD - (CONV_K - 1)
    pieces = [(cb, r0) for cb in range(d // LANES) for r0 in range(0, tm, CONV_RB)]
    n_pieces = len(pieces)
    n_chunks = d_ff // FF_CHUNK
    n_groups = -(-n_chunks // FF_GROUP)
    wide, narrow = d // FF_CHUNK, 1
    budget = PRE_FFN_UNITS + 2 * (n_chunks - 1) * narrow + (n_groups - 1) * wide
    spent = [0]
    yc_c0, hb_c0 = 0, d
    gate_row = pl.multiple_of(row0, 16)

    def gated(c0):
        return mix_ref[pl.ds(gate_row, tm), c0:c0 + d]

    def conv_pieces(cost):
        lo = min(spent[0], budget) * n_pieces // budget
        spent[0] += cost
        hi = min(spent[0], budget) * n_pieces // budget
        for _ in range(hi - lo):
            cb, r0 = pieces.pop(0)
            csl = slice(cb * LANES, (cb + 1) * LANES)
            acc = jnp.broadcast_to(cb_ref[:, csl], (CONV_RB, LANES))
            for res in range(8):
                taps = list(range(res, CONV_K, 8))
                n_rows = CONV_RB + 8 * (len(taps) - 1)
                win = buf_ref[cb, pl.ds(row0 + (r0 + base + res), n_rows), :]
                for t, kk in enumerate(taps):
                    acc = acc + cw_ref[kk:kk + 1, csl] * win[8 * t:8 * t + CONV_RB]
            mix_ref[r0:r0 + CONV_RB, yc_c0 + cb * LANES:yc_c0 + (cb + 1) * LANES] = acc.astype(BF16)

    yc = mix_ref[:, yc_c0:yc_c0 + d].astype(F32)
    mu = jnp.mean(yc, axis=-1, keepdims=True)
    xc = yc - mu
    var = jnp.mean(xc * xc, axis=-1, keepdims=True)
    z = xc * lax.rsqrt(var + EPS) * lng_ref[...] + lnb_ref[...]
    z = (z * _sigmoid(z)).astype(BF16)

    y_a = _dot(o_ref[...], wao_ref[...])
    y_b = _dot(z, wco_ref[...])
    merged = sga_ref[...].astype(F32) * y_a + sgb_ref[...].astype(F32) * y_b
    mo = _dot(merged.astype(BF16), wout_ref[...])
    buf_ref[0, CONV_PAD + tm:, :] = y_b[-CONV_SPARE:, -LANES:]
    conv_pieces(PRE_FFN_UNITS)
    x1 = x_ref[...] + _rms(mo, gpost_ref[...])

    mix_ref[:, hb_c0:hb_c0 + d] = _rms(x1, gfpre_ref[...]).astype(BF16)
    f = None
    for grp in range(n_groups):
        g0 = grp * FF_GROUP * FF_CHUNK
        g1 = min(g0 + FF_GROUP * FF_CHUNK, d_ff)
        for c0 in range(g0, g1, FF_CHUNK):
            if (c0 // FF_CHUNK) % 2 == 0 or c0 == d_ff - FF_CHUNK:
                hb = gated(hb_c0)
            g = _dot(hb, wfin_ref[:, c0:c0 + FF_CHUNK])
            conv_pieces(narrow)
            up = _dot(hb, wfin_ref[:, d_ff + c0:d_ff + c0 + FF_CHUNK])
            conv_pieces(narrow)
            act_ref[:, c0:c0 + FF_CHUNK] = ((g + g * jnp.tanh(g)) * up).astype(BF16)
        part = _dot(act_ref[:, g0:g1], wfout_ref[g0:g1, :])
        conv_pieces(wide)
        f = part if f is None else f + part
    assert not pieces
    out_ref[...] = x1 + _rms(f, gfpost_ref[...])


def _tail(x2d, o2d, sga, sgb, u2d, mhalo, weights, tm, seq):
    rows, d = x2d.shape
    d_ff = weights[10].shape[0]
    n_tiles = rows // tm
    halo_per_tile = tm // CONV_PAD
    lag_spec = pl.BlockSpec((tm, d), lambda s: (jnp.maximum(s - 1, 0), 0))
    conv_tile = lambda s: jnp.minimum(s, n_tiles - 1)
    return pl.pallas_call(
        functools.partial(_tail_kernel, tiles_per_seq=seq // tm, n_tiles=n_tiles),
        grid=(n_tiles + 1,),
        in_specs=[lag_spec] * 4 + [
            pl.BlockSpec((tm, d), lambda s: (conv_tile(s), 0)),
            pl.BlockSpec((CONV_PAD, d),
                         lambda s: (jnp.maximum(conv_tile(s) * halo_per_tile - 1, 0), 0)),
        ] + [_const_spec(mhalo.shape)] + [_const_spec(w.shape) for w in weights],
        out_specs=lag_spec,
        out_shape=jax.ShapeDtypeStruct((rows, d), F32),
        scratch_shapes=[pltpu.VMEM((d // LANES, tm + CONV_PAD + CONV_SPARE, LANES), F32),
                        pltpu.VMEM((tm, 2 * d), BF16),
                        pltpu.VMEM((tm, d_ff), BF16)],
        compiler_params=pltpu.CompilerParams(
            dimension_semantics=("arbitrary",), vmem_limit_bytes=VMEM_LIMIT),
        name="tail",
    )(x2d, o2d, sga, sgb, u2d, u2d, mhalo, *weights)


def _rope_tables(length):
    pos = np.arange(length, dtype=np.float32)
    inv = np.float32(ROPE_BASE) ** (-np.arange(0, QK_ROPE, 2, dtype=np.float32) / QK_ROPE)
    ang = (pos[:, None] * inv[None, :]).astype(np.float32)
    cos, sin = np.cos(ang), np.sin(ang)
    zeros = lambda w: np.zeros((length, w), np.float32)
    ktail = LANES - QK_ROPE
    ck = np.concatenate([cos, cos, zeros(ktail)], axis=1)
    sk_lo = np.concatenate([-sin, zeros(HALF_ROPE + ktail)], axis=1)
    sk_hi = np.concatenate([zeros(HALF_ROPE), sin, zeros(ktail)], axis=1)
    tabs = np.stack([ck, sk_lo, sk_hi]).astype(np.float32)
    tabs_t = (np.stack([cos.T, sin.T]) * np.float32(Q_SCALE)).astype(np.float32)
    return tabs, tabs_t


def kernel(x, meta, mix_pre_g, w_in, q_norm_g, w_uq, kv_norm_g, w_ukv, w_attn_o,
           conv_w, conv_b, conv_ln_g, conv_ln_b, w_conv_o, w_out, mix_post_g,
           ffn_pre_g, w_ffn_in, w_ffn_out, ffn_post_g):
    assert w_in.shape[0] == 1, "single-layer block"
    b, s, d = x.shape
    row = lambda g: g[0][None, :].astype(F32)

    w_in0 = w_in[0]
    wsm, wbig = _inweights(w_in0, 4 * LANES)

    qk_dim = QK_NOPE + QK_ROPE
    wq = w_uq[0].reshape(Q_RANK, N_HEADS, qk_dim)
    wq = jnp.pad(wq, ((0, 0), (0, 0), (0, HEAD_W - qk_dim)))
    wqt = wq.reshape(Q_RANK, N_HEADS * HEAD_W).T.astype(BF16)

    wkv = w_ukv[0].reshape(KV_RANK, N_HEADS, QK_NOPE + V_DIM)
    wk_nope = jnp.pad(wkv[:, :, :QK_NOPE], ((0, 0), (0, 0), (0, HEAD_W - QK_NOPE)))
    place = np.zeros((LANES, N_HEADS, HEAD_W), np.float32)
    jj = np.arange(QK_ROPE)
    place[jj, :, QK_NOPE + jj] = 1.0
    wk = jnp.concatenate([wk_nope, jnp.asarray(place)], axis=0)
    wk = wk.reshape(KV_RANK + LANES, N_HEADS * HEAD_W).astype(BF16)
    wvt = wkv[:, :, QK_NOPE:].reshape(KV_RANK, N_HEADS * V_DIM).T.astype(BF16)

    in_weights = (row(mix_pre_g), wsm, wbig, row(q_norm_g), wqt, row(kv_norm_g), wk, wvt)

    tabs, tabs_t = _rope_tables(N_META + s)

    tm = 256
    x2d = x.reshape(b * s, d)
    qt, k, vt, u, sga, sgb = _inproj(x2d, tabs[:, N_META:], tabs_t[:, :, N_META:],
                                     in_weights, 2 * tm, s)
    _, km, vmt, um, _, _ = _inproj(meta.astype(F32), tabs[:, :N_META],
                                   tabs_t[:, :, :N_META], in_weights, N_META, N_META)

    o = _attention(qt, k.reshape(b, s, -1), vt, km, vmt[0], tq=256)

    mhalo = jnp.concatenate([jnp.zeros((CONV_PAD - N_META, d), BF16), um], axis=0)
    d_ff = w_ffn_out.shape[1]
    ffn_scale = jnp.concatenate([jnp.full((d_ff,), 0.5, F32), jnp.ones((d_ff,), F32)])
    tail_weights = (conv_w[0].astype(F32), row(conv_b), row(conv_ln_g), row(conv_ln_b),
                    w_attn_o[0].astype(BF16), w_conv_o[0].astype(BF16),
                    w_out[0].astype(BF16), row(mix_post_g), row(ffn_pre_g),
                    (w_ffn_in[0] * ffn_scale).astype(BF16), w_ffn_out[0].astype(BF16),
                    row(ffn_post_g))
    out = _tail(x2d, o.reshape(b * s, -1), sga, sgb, u, mhalo, tail_weights, tm, s)
    return out.reshape(b, s, d)
```

```python
import functools
import math

import jax
import jax.numpy as jnp
import numpy as np
from jax import lax
from jax.experimental import pallas as pl
from jax.experimental.pallas import tpu as pltpu

CHUNK = 64
N_META = 16
N_HEADS = 16
QK_NOPE = 64
QK_ROPE = 32
V_DIM = 64
Q_RANK = 256
KV_RANK = 128
ROPE_BASE = 10000.0
CONV_K = 31
EPS = 1e-6
NEG_INF = -1e30

LANES = 128
VMEM_LIMIT = 56 * 1024 * 1024

HEAD_W = LANES
HALF_ROPE = QK_ROPE // 2
V_ROWS = V_DIM + 16
Q_SCALE = math.log2(math.e) / math.sqrt(QK_NOPE + QK_ROPE)

BF16 = jnp.bfloat16
F32 = jnp.float32


def _rms(x, g):
    ms = jnp.mean(x * x, axis=-1, keepdims=True)
    return x * lax.rsqrt(ms + EPS) * g


def _sigmoid(x):
    return 0.5 * (jnp.tanh(0.5 * x) + 1.0)


def _sigmoid_of_twice(h):
    return 0.5 * jnp.tanh(h) + 0.5


def _dot(a, b):
    return jnp.dot(a, b, preferred_element_type=F32)


def _dot_t(a, b):
    return lax.dot_general(a, b, (((1,), (1,)), ((), ())),
                           preferred_element_type=F32)


def _rope(g, c, s_lo, s_hi):
    up = pltpu.roll(g, LANES - HALF_ROPE, 1)
    dn = pltpu.roll(g, HALF_ROPE, 1)
    return g * c + up * s_lo + dn * s_hi


def _inproj_kernel(x_ref, tab_ref, tabt_ref, gpre_ref, wint_ref, gq_ref,
                   wqt_ref, gkv_ref, wk_ref, wvt_ref,
                   qt_ref, k_ref, vt_ref, u_ref, sga_ref, sgb_ref):
    d = x_ref.shape[1]
    xb = _rms(x_ref[...], gpre_ref[...]).astype(BF16)

    off = Q_RANK + KV_RANK + QK_ROPE
    small = _dot_t(xb, wint_ref[0:4 * LANES, :])
    a = _dot_t(xb, wint_ref[off:off + d, :])
    g = _dot_t(xb, wint_ref[off + d:off + 2 * d, :])
    u_ref[...] = (a * _sigmoid_of_twice(g)).astype(BF16)
    qn = _rms(small[:, :Q_RANK], gq_ref[...]).astype(BF16)
    kvn = _rms(small[:, Q_RANK:Q_RANK + KV_RANK], gkv_ref[...]).astype(BF16)
    kr = _rope(small[:, Q_RANK + KV_RANK:], tab_ref[0], tab_ref[1], tab_ref[2])

    kin = jnp.concatenate([kvn, kr.astype(BF16)], axis=1)
    k_ref[...] = _dot(kin, wk_ref[...]).astype(BF16)
    vt = _dot_t(wvt_ref[...], kvn).astype(BF16)
    for h in range(N_HEADS):
        vt_ref[h * V_ROWS:h * V_ROWS + V_DIM, :] = vt[h * V_DIM:(h + 1) * V_DIM]
        vt_ref[h * V_ROWS + V_DIM:(h + 1) * V_ROWS, :] = jnp.ones(
            (V_ROWS - V_DIM, vt.shape[1]), BF16)

    qt = _dot_t(wqt_ref[...], qn)
    cos_t, sin_t = tabt_ref[0], tabt_ref[1]
    scale = Q_SCALE
    for h in range(N_HEADS):
        r0 = h * HEAD_W
        lo = qt[r0 + QK_NOPE:r0 + QK_NOPE + HALF_ROPE]
        hi = qt[r0 + QK_NOPE + HALF_ROPE:r0 + QK_NOPE + QK_ROPE]
        qt_ref[r0:r0 + QK_NOPE, :] = (qt[r0:r0 + QK_NOPE] * scale).astype(BF16)
        qt_ref[r0 + QK_NOPE:r0 + QK_NOPE + QK_ROPE, :] = jnp.concatenate(
            [lo * cos_t - hi * sin_t, hi * cos_t + lo * sin_t], axis=0).astype(BF16)
        qt_ref[r0 + QK_NOPE + QK_ROPE:r0 + HEAD_W, :] = jnp.zeros(
            (HEAD_W - QK_NOPE - QK_ROPE, qt.shape[1]), BF16)

    sga_ref[...] = _sigmoid_of_twice(
        _dot_t(xb, wint_ref[off + 2 * d:off + 3 * d, :])).astype(BF16)
    sgb_ref[...] = _sigmoid_of_twice(
        _dot_t(xb, wint_ref[off + 3 * d:off + 4 * d, :])).astype(BF16)


def _const_spec(shape):
    nd = len(shape)
    return pl.BlockSpec(shape, lambda *_: (0,) * nd, pipeline_mode=pl.Buffered(1))


def _inproj(x2d, tabs, tabs_t, weights, tm, seq):
    rows, d = x2d.shape
    gpre, wint, gq, wqt, gkv, wk, wvt = weights
    n_q = wqt.shape[0]
    n_v = N_HEADS * V_ROWS
    tps = seq // tm
    row_spec = lambda w: pl.BlockSpec((tm, w), lambda i: (i, 0))
    col_spec = lambda h: pl.BlockSpec((None, h, tm), lambda i: (i // tps, 0, i % tps))
    out_shapes = [
        jax.ShapeDtypeStruct((rows // seq, n_q, seq), BF16),
        jax.ShapeDtypeStruct((rows, n_q), BF16),
        jax.ShapeDtypeStruct((rows // seq, n_v, seq), BF16),
        jax.ShapeDtypeStruct((rows, d), BF16),
        jax.ShapeDtypeStruct((rows, d), BF16),
        jax.ShapeDtypeStruct((rows, d), BF16),
    ]
    return pl.pallas_call(
        _inproj_kernel,
        grid=(rows // tm,),
        in_specs=[
            row_spec(d),
            pl.BlockSpec((3, tm, LANES), lambda i: (0, i % tps, 0)),
            pl.BlockSpec((2, HALF_ROPE, tm), lambda i: (0, 0, i % tps)),
            _const_spec(gpre.shape), _const_spec(wint.shape), _const_spec(gq.shape),
            _const_spec(wqt.shape), _const_spec(gkv.shape),
            _const_spec(wk.shape), _const_spec(wvt.shape),
        ],
        out_specs=[col_spec(n_q), row_spec(n_q), col_spec(n_v),
                   row_spec(d), row_spec(d), row_spec(d)],
        out_shape=out_shapes,
        compiler_params=pltpu.CompilerParams(
            dimension_semantics=("parallel",), vmem_limit_bytes=VMEM_LIMIT),
        name="inproj",
    )(x2d, tabs, tabs_t, gpre, wint, gq, wqt, gkv, wk, wvt)


ATTN_LOOKAHEAD = 6
ATTN_HEADS = 4


def _attn_kernel(qt_ref, k_ref, vt_ref, km_ref, vmt_ref, o_ref, *, tq):
    s_len = k_ref.shape[0]
    key_chunk = lax.broadcasted_iota(jnp.int32, (tq, tq), 0) // CHUNK
    qry_chunk = lax.broadcasted_iota(jnp.int32, (tq, tq), 1) // CHUNK
    diag_ok = key_chunk <= qry_chunk

    items = []
    for i in reversed(range(s_len // tq)):
        for h in range(ATTN_HEADS):
            items += [(i, h, c) for c in [None] + list(range(i + 1))]

    def scores(i, h, c):
        ksl = slice(h * HEAD_W, (h + 1) * HEAD_W)
        qt = qt_ref[ksl, i * tq:(i + 1) * tq]
        if c is None:
            return _dot(km_ref[:, ksl], qt)
        s = _dot(k_ref[c * tq:(c + 1) * tq, ksl], qt)
        return jnp.where(diag_ok, s, NEG_INF) if c == i else s

    def update(state, s, h, c):
        vsl = slice(h * V_ROWS, (h + 1) * V_ROWS)
        cm = jnp.max(s, axis=0, keepdims=True)
        m_new = cm if state is None else jnp.maximum(state[0], cm)
        p = jnp.exp2(s - m_new).astype(BF16)
        v = vmt_ref[vsl, :] if c is None else vt_ref[vsl, c * tq:(c + 1) * tq]
        pv = _dot(v, p)
        if state is None:
            return m_new, pv
        return m_new, jnp.exp2(state[0] - m_new) * state[1] + pv

    pending = [scores(*items[n]) for n in range(min(ATTN_LOOKAHEAD, len(items)))]
    states, outs = {}, {}
    for n, (i, h, c) in enumerate(items):
        if n + ATTN_LOOKAHEAD < len(items):
            pending.append(scores(*items[n + ATTN_LOOKAHEAD]))
        states[h] = update(states.get(h), pending.pop(0), h, c)
        if c == i:
            acc = states.pop(h)[1]
            outs[h] = acc[:V_DIM] * (1.0 / acc[V_DIM:V_DIM + 1])
            if h % 2 == 1:
                ot = jnp.concatenate([outs[h - 1], outs[h]], axis=0)
                o_ref[i * tq:(i + 1) * tq, (h - 1) * V_DIM:(h + 1) * V_DIM] = ot.T.astype(o_ref.dtype)


def _attention(qt, k, vt, km, vmt, tq):
    b, s, _ = k.shape
    g = ATTN_HEADS
    return pl.pallas_call(
        functools.partial(_attn_kernel, tq=tq),
        grid=(b, N_HEADS // g),
        in_specs=[
            pl.BlockSpec((None, g * HEAD_W, s), lambda bi, hp: (bi, hp, 0)),
            pl.BlockSpec((None, s, g * HEAD_W), lambda bi, hp: (bi, 0, hp)),
            pl.BlockSpec((None, g * V_ROWS, s), lambda bi, hp: (bi, hp, 0)),
            pl.BlockSpec((N_META, g * HEAD_W), lambda bi, hp: (0, hp)),
            pl.BlockSpec((g * V_ROWS, N_META), lambda bi, hp: (hp, 0)),
        ],
        out_specs=pl.BlockSpec((None, s, g * V_DIM), lambda bi, hp: (bi, 0, hp)),
        out_shape=jax.ShapeDtypeStruct((b, s, N_HEADS * V_DIM), BF16),
        compiler_params=pltpu.CompilerParams(
            dimension_semantics=("parallel", "parallel"),
            vmem_limit_bytes=VMEM_LIMIT),
        name="attention",
    )(qt, k, vt, km, vmt)


CONV_PAD = 32
CONV_RB = 64
FF_CHUNK = 256
FF_GROUP = 6
PRE_FFN_UNITS = 3
CONV_SPARE = 8


def _tail_kernel(x_ref, o_ref, sga_ref, sgb_ref, u_ref, halo_ref, mhalo_ref,
                 cw_ref, cb_ref, lng_ref, lnb_ref, wao_ref, wco_ref, wout_ref,
                 gpost_ref, gfpre_ref, wfin_ref, wfout_ref, gfpost_ref,
                 out_ref, buf_ref, mix_ref, act_ref, *, tiles_per_seq, n_tiles):
    s = pl.program_id(0)
    tm, d = u_ref.shape
    d_ff = wfout_ref.shape[0]

    @pl.when(s == 0)
    def _():
        mix_ref[...] = jnp.zeros(mix_ref.shape, mix_ref.dtype)

    conv_tile = jnp.minimum(s, n_tiles - 1)
    first_in_seq = conv_tile % tiles_per_seq == 0
    row0 = jnp.minimum(s, 0)

    hist = jnp.where(first_in_seq, mhalo_ref[...], halo_ref[...]).astype(F32)
    for cb in range(d // LANES):
        csl = slice(cb * LANES, (cb + 1) * LANES)
        buf_ref[cb, 0:CONV_PAD, :] = hist[:, csl]
        buf_ref[cb, CONV_PAD:CONV_PAD + tm, :] = u_ref[:, csl].astype(F32)

    base = CONV_PAD - (CONV_K - 1)
    pieces = [(cb, r0) for cb in range(d // LANES) for r0 in range(0, tm, CONV_RB)]
    n_pieces = len(pieces)
    n_chunks = d_ff // FF_CHUNK
    n_groups = -(-n_chunks // FF_GROUP)
    wide, narrow = d // FF_CHUNK, 1
    budget = PRE_FFN_UNITS + 2 * (n_chunks - 1) * narrow + (n_groups - 1) * wide
    spent = [0]
    yc_c0, hb_c0 = 0, d
    gate_row = pl.multiple_of(row0, 16)

    def gated(c0):
        return mix_ref[pl.ds(gate_row, tm), c0:c0 + d]

    def conv_pieces(cost):
        lo = min(spent[0], budget) * n_pieces // budget
        spent[0] += cost
        hi = min(spent[0], budget) * n_pieces // budget
        for _ in range(hi - lo):
            cb, r0 = pieces.pop(0)
            csl = slice(cb * LANES, (cb + 1) * LANES)
            acc = jnp.broadcast_to(cb_ref[:, csl], (CONV_RB, LANES))
            for res in range(8):
                taps = list(range(res, CONV_K, 8))
                n_rows = CONV_RB + 8 * (len(taps) - 1)
                win = buf_ref[cb, pl.ds(row0 + (r0 + base + res), n_rows), :]
                for t, kk in enumerate(taps):
                    acc = acc + cw_ref[kk:kk + 1, csl] * win[8 * t:8 * t + CONV_RB]
            mix_ref[r0:r0 + CONV_RB, yc_c0 + cb * LANES:yc_c0 + (cb + 1) * LANES] = acc.astype(BF16)

    yc = mix_ref[:, yc_c0:yc_c0 + d].astype(F32)
    mu = jnp.mean(yc, axis=-1, keepdims=True)
    xc = yc - mu
    var = jnp.mean(xc * xc, axis=-1, keepdims=True)
    z = xc * lax.rsqrt(var + EPS) * lng_ref[...] + lnb_ref[...]
    z = (z * _sigmoid(z)).astype(BF16)

    y_a = _dot(o_ref[...], wao_ref[...])
    y_b = _dot(z, wco_ref[...])
    merged = sga_ref[...].astype(F32) * y_a + sgb_ref[...].astype(F32) * y_b
    mo = _dot(merged.astype(BF16), wout_ref[...])
    buf_ref[0, CONV_PAD + tm:, :] = y_b[-CONV_SPARE:, -LANES:]
    conv_pieces(PRE_FFN_UNITS)
    x1 = x_ref[...] + _rms(mo, gpost_ref[...])

    mix_ref[:, hb_c0:hb_c0 + d] = _rms(x1, gfpre_ref[...]).astype(BF16)
    f = None
    for grp in range(n_groups):
        g0 = grp * FF_GROUP * FF_CHUNK
        g1 = min(g0 + FF_GROUP * FF_CHUNK, d_ff)
        for c0 in range(g0, g1, FF_CHUNK):
            if (c0 // FF_CHUNK) % 2 == 0 or c0 == d_ff - FF_CHUNK:
                hb = gated(hb_c0)
            g = _dot(hb, wfin_ref[:, c0:c0 + FF_CHUNK])
            conv_pieces(narrow)
            up = _dot(hb, wfin_ref[:, d_ff + c0:d_ff + c0 + FF_CHUNK])
            conv_pieces(narrow)
            act_ref[:, c0:c0 + FF_CHUNK] = ((g + g * jnp.tanh(g)) * up).astype(BF16)
        part = _dot(act_ref[:, g0:g1], wfout_ref[g0:g1, :])
        conv_pieces(wide)
        f = part if f is None else f + part
    assert not pieces
    out_ref[...] = x1 + _rms(f, gfpost_ref[...])


def _tail(x2d, o2d, sga, sgb, u2d, mhalo, weights, tm, seq):
    rows, d = x2d.shape
    d_ff = weights[10].shape[0]
    n_tiles = rows // tm
    halo_per_tile = tm // CONV_PAD
    lag_spec = pl.BlockSpec((tm, d), lambda s: (jnp.maximum(s - 1, 0), 0))
    conv_tile = lambda s: jnp.minimum(s, n_tiles - 1)
    return pl.pallas_call(
        functools.partial(_tail_kernel, tiles_per_seq=seq // tm, n_tiles=n_tiles),
        grid=(n_tiles + 1,),
        in_specs=[lag_spec] * 4 + [
            pl.BlockSpec((tm, d), lambda s: (conv_tile(s), 0)),
            pl.BlockSpec((CONV_PAD, d),
                         lambda s: (jnp.maximum(conv_tile(s) * halo_per_tile - 1, 0), 0)),
        ] + [_const_spec(mhalo.shape)] + [_const_spec(w.shape) for w in weights],
        out_specs=lag_spec,
        out_shape=jax.ShapeDtypeStruct((rows, d), F32),
        scratch_shapes=[pltpu.VMEM((d // LANES, tm + CONV_PAD + CONV_SPARE, LANES), F32),
                        pltpu.VMEM((tm, 2 * d), BF16),
                        pltpu.VMEM((tm, d_ff), BF16)],
        compiler_params=pltpu.CompilerParams(
            dimension_semantics=("arbitrary",), vmem_limit_bytes=VMEM_LIMIT),
        name="tail",
    )(x2d, o2d, sga, sgb, u2d, u2d, mhalo, *weights)


def _rope_tables(length):
    pos = np.arange(length, dtype=np.float32)
    inv = np.float32(ROPE_BASE) ** (-np.arange(0, QK_ROPE, 2, dtype=np.float32) / QK_ROPE)
    ang = (pos[:, None] * inv[None, :]).astype(np.float32)
    cos, sin = np.cos(ang), np.sin(ang)
    zeros = lambda w: np.zeros((length, w), np.float32)
    ktail = LANES - QK_ROPE
    ck = np.concatenate([cos, cos, zeros(ktail)], axis=1)
    sk_lo = np.concatenate([-sin, zeros(HALF_ROPE + ktail)], axis=1)
    sk_hi = np.concatenate([zeros(HALF_ROPE), sin, zeros(ktail)], axis=1)
    tabs = np.stack([ck, sk_lo, sk_hi]).astype(np.float32)
    tabs_t = (np.stack([cos.T, sin.T]) * np.float32(Q_SCALE)).astype(np.float32)
    return tabs, tabs_t


def kernel(x, meta, mix_pre_g, w_in, q_norm_g, w_uq, kv_norm_g, w_ukv, w_attn_o,
           conv_w, conv_b, conv_ln_g, conv_ln_b, w_conv_o, w_out, mix_post_g,
           ffn_pre_g, w_ffn_in, w_ffn_out, ffn_post_g):
    assert w_in.shape[0] == 1, "single-layer block"
    b, s, d = x.shape
    row = lambda g: g[0][None, :].astype(F32)

    w_in0 = w_in[0]
    n_in = w_in0.shape[1]
    col_scale = jnp.concatenate([jnp.ones((n_in - 3 * d,), F32), jnp.full((3 * d,), 0.5, F32)])
    wint = (w_in0.T * col_scale[:, None]).astype(BF16)

    qk_dim = QK_NOPE + QK_ROPE
    wq = w_uq[0].reshape(Q_RANK, N_HEADS, qk_dim)
    wq = jnp.pad(wq, ((0, 0), (0, 0), (0, HEAD_W - qk_dim)))
    wqt = wq.reshape(Q_RANK, N_HEADS * HEAD_W).T.astype(BF16)

    wkv = w_ukv[0].reshape(KV_RANK, N_HEADS, QK_NOPE + V_DIM)
    wk_nope = jnp.pad(wkv[:, :, :QK_NOPE], ((0, 0), (0, 0), (0, HEAD_W - QK_NOPE)))
    place = np.zeros((LANES, N_HEADS, HEAD_W), np.float32)
    jj = np.arange(QK_ROPE)
    place[jj, :, QK_NOPE + jj] = 1.0
    wk = jnp.concatenate([wk_nope, jnp.asarray(place)], axis=0)
    wk = wk.reshape(KV_RANK + LANES, N_HEADS * HEAD_W).astype(BF16)
    wvt = wkv[:, :, QK_NOPE:].reshape(KV_RANK, N_HEADS * V_DIM).T.astype(BF16)

    in_weights = (row(mix_pre_g), wint, row(q_norm_g), wqt, row(kv_norm_g), wk, wvt)

    tabs, tabs_t = _rope_tables(N_META + s)

    tm = 256
    x2d = x.reshape(b * s, d)
    qt, k, vt, u, sga, sgb = _inproj(x2d, tabs[:, N_META:], tabs_t[:, :, N_META:],
                                     in_weights, 2 * tm, s)
    _, km, vmt, um, _, _ = _inproj(meta.astype(F32), tabs[:, :N_META],
                                   tabs_t[:, :, :N_META], in_weights, N_META, N_META)

    o = _attention(qt, k.reshape(b, s, -1), vt, km, vmt[0], tq=256)

    mhalo = jnp.concatenate([jnp.zeros((CONV_PAD - N_META, d), BF16), um], axis=0)
    d_ff = w_ffn_out.shape[1]
    ffn_scale = jnp.concatenate([jnp.full((d_ff,), 0.5, F32), jnp.ones((d_ff,), F32)])
    tail_weights = (conv_w[0].astype(F32), row(conv_b), row(conv_ln_g), row(conv_ln_b),
                    w_attn_o[0].astype(BF16), w_conv_o[0].astype(BF16),
                    w_out[0].astype(BF16), row(mix_post_g), row(ffn_pre_g),
                    (w_ffn_in[0] * ffn_scale).astype(BF16), w_ffn_out[0].astype(BF16),
                    row(ffn_post_g))
    out = _tail(x2d, o.reshape(b * s, -1), sga, sgb, u, mhalo, tail_weights, tm, s)
    return out.reshape(b, s, d)
```

```python
import functools
import math

import jax
import jax.numpy as jnp
import numpy as np
from jax import lax
from jax.experimental import pallas as pl
from jax.experimental.pallas import tpu as pltpu

CHUNK = 64
N_META = 16
N_HEADS = 16
QK_NOPE = 64
QK_ROPE = 32
V_DIM = 64
Q_RANK = 256
KV_RANK = 128
ROPE_BASE = 10000.0
CONV_K = 31
EPS = 1e-6
NEG_INF = -1e30

LANES = 128
VMEM_LIMIT = 56 * 1024 * 1024

HEAD_W = LANES
HALF_ROPE = QK_ROPE // 2
V_ROWS = V_DIM + 16
Q_SCALE = math.log2(math.e) / math.sqrt(QK_NOPE + QK_ROPE)

BF16 = jnp.bfloat16
F32 = jnp.float32


def _rms(x, g):
    ms = jnp.mean(x * x, axis=-1, keepdims=True)
    return x * lax.rsqrt(ms + EPS) * g


def _sigmoid(x):
    return 0.5 * (jnp.tanh(0.5 * x) + 1.0)


def _sigmoid_of_twice(h):
    return 0.5 * jnp.tanh(h) + 0.5


def _dot(a, b):
    return jnp.dot(a, b, preferred_element_type=F32)


def _dot_t(a, b):
    return lax.dot_general(a, b, (((1,), (1,)), ((), ())),
                           preferred_element_type=F32)


def _rope(g, c, s_lo, s_hi):
    up = pltpu.roll(g, LANES - HALF_ROPE, 1)
    dn = pltpu.roll(g, HALF_ROPE, 1)
    return g * c + up * s_lo + dn * s_hi


def _inproj_kernel(x_ref, tab_ref, tabt_ref, gpre_ref, wint_ref, gq_ref,
                   wqt_ref, gkv_ref, wk_ref, wvt_ref,
                   qt_ref, k_ref, vt_ref, u_ref, sga_ref, sgb_ref):
    d = x_ref.shape[1]
    xb = _rms(x_ref[...], gpre_ref[...]).astype(BF16)

    off = Q_RANK + KV_RANK + QK_ROPE
    small = _dot_t(xb, wint_ref[0:4 * LANES, :])
    a = _dot_t(xb, wint_ref[off:off + d, :])
    g = _dot_t(xb, wint_ref[off + d:off + 2 * d, :])
    u_ref[...] = (a * _sigmoid_of_twice(g)).astype(BF16)
    qn = _rms(small[:, :Q_RANK], gq_ref[...]).astype(BF16)
    kvn = _rms(small[:, Q_RANK:Q_RANK + KV_RANK], gkv_ref[...]).astype(BF16)
    kr = _rope(small[:, Q_RANK + KV_RANK:], tab_ref[0], tab_ref[1], tab_ref[2])

    kin = jnp.concatenate([kvn, kr.astype(BF16)], axis=1)
    k_ref[...] = _dot(kin, wk_ref[...]).astype(BF16)
    vt = _dot_t(wvt_ref[...], kvn).astype(BF16)
    for h in range(N_HEADS):
        vt_ref[h * V_ROWS:h * V_ROWS + V_DIM, :] = vt[h * V_DIM:(h + 1) * V_DIM]
        vt_ref[h * V_ROWS + V_DIM:(h + 1) * V_ROWS, :] = jnp.ones(
            (V_ROWS - V_DIM, vt.shape[1]), BF16)

    qt = _dot_t(wqt_ref[...], qn)
    cos_t, sin_t = tabt_ref[0], tabt_ref[1]
    scale = Q_SCALE
    for h in range(N_HEADS):
        r0 = h * HEAD_W
        lo = qt[r0 + QK_NOPE:r0 + QK_NOPE + HALF_ROPE]
        hi = qt[r0 + QK_NOPE + HALF_ROPE:r0 + QK_NOPE + QK_ROPE]
        qt_ref[r0:r0 + QK_NOPE, :] = (qt[r0:r0 + QK_NOPE] * scale).astype(BF16)
        qt_ref[r0 + QK_NOPE:r0 + QK_NOPE + QK_ROPE, :] = jnp.concatenate(
            [lo * cos_t - hi * sin_t, hi * cos_t + lo * sin_t], axis=0).astype(BF16)
        qt_ref[r0 + QK_NOPE + QK_ROPE:r0 + HEAD_W, :] = jnp.zeros(
            (HEAD_W - QK_NOPE - QK_ROPE, qt.shape[1]), BF16)

    sga_ref[...] = _sigmoid_of_twice(
        _dot_t(xb, wint_ref[off + 2 * d:off + 3 * d, :])).astype(BF16)
    sgb_ref[...] = _sigmoid_of_twice(
        _dot_t(xb, wint_ref[off + 3 * d:off + 4 * d, :])).astype(BF16)


def _const_spec(shape):
    nd = len(shape)
    return pl.BlockSpec(shape, lambda *_: (0,) * nd, pipeline_mode=pl.Buffered(1))


def _inproj(x2d, tabs, tabs_t, weights, tm, seq):
    rows, d = x2d.shape
    gpre, wint, gq, wqt, gkv, wk, wvt = weights
    n_q = wqt.shape[0]
    n_v = N_HEADS * V_ROWS
    tps = seq // tm
    row_spec = lambda w: pl.BlockSpec((tm, w), lambda i: (i, 0))
    col_spec = lambda h: pl.BlockSpec((None, h, tm), lambda i: (i // tps, 0, i % tps))
    out_shapes = [
        jax.ShapeDtypeStruct((rows // seq, n_q, seq), BF16),
        jax.ShapeDtypeStruct((rows, n_q), BF16),
        jax.ShapeDtypeStruct((rows // seq, n_v, seq), BF16),
        jax.ShapeDtypeStruct((rows, d), BF16),
        jax.ShapeDtypeStruct((rows, d), BF16),
        jax.ShapeDtypeStruct((rows, d), BF16),
    ]
    return pl.pallas_call(
        _inproj_kernel,
        grid=(rows // tm,),
        in_specs=[
            row_spec(d),
            pl.BlockSpec((3, tm, LANES), lambda i: (0, i % tps, 0)),
            pl.BlockSpec((2, HALF_ROPE, tm), lambda i: (0, 0, i % tps)),
            _const_spec(gpre.shape), _const_spec(wint.shape), _const_spec(gq.shape),
            _const_spec(wqt.shape), _const_spec(gkv.shape),
            _const_spec(wk.shape), _const_spec(wvt.shape),
        ],
        out_specs=[col_spec(n_q), row_spec(n_q), col_spec(n_v),
                   row_spec(d), row_spec(d), row_spec(d)],
        out_shape=out_shapes,
        compiler_params=pltpu.CompilerParams(
            dimension_semantics=("parallel",), vmem_limit_bytes=VMEM_LIMIT),
        name="inproj",
    )(x2d, tabs, tabs_t, gpre, wint, gq, wqt, gkv, wk, wvt)


ATTN_LOOKAHEAD = 6
ATTN_HEADS = 4


def _attn_kernel(qt_ref, k_ref, vt_ref, km_ref, vmt_ref, o_ref, *, tq):
    s_len = k_ref.shape[0]
    key_chunk = lax.broadcasted_iota(jnp.int32, (tq, tq), 0) // CHUNK
    qry_chunk = lax.broadcasted_iota(jnp.int32, (tq, tq), 1) // CHUNK
    diag_ok = key_chunk <= qry_chunk

    items = []
    for i in reversed(range(s_len // tq)):
        for h in range(ATTN_HEADS):
            items += [(i, h, c) for c in [None] + list(range(i + 1))]

    def scores(i, h, c):
        ksl = slice(h * HEAD_W, (h + 1) * HEAD_W)
        qt = qt_ref[ksl, i * tq:(i + 1) * tq]
        if c is None:
            return _dot(km_ref[:, ksl], qt)
        s = _dot(k_ref[c * tq:(c + 1) * tq, ksl], qt)
        return jnp.where(diag_ok, s, NEG_INF) if c == i else s

    def update(state, s, h, c):
        vsl = slice(h * V_ROWS, (h + 1) * V_ROWS)
        cm = jnp.max(s, axis=0, keepdims=True)
        m_new = cm if state is None else jnp.maximum(state[0], cm)
        p = jnp.exp2(s - m_new).astype(BF16)
        v = vmt_ref[vsl, :] if c is None else vt_ref[vsl, c * tq:(c + 1) * tq]
        pv = _dot(v, p)
        if state is None:
            return m_new, pv
        return m_new, jnp.exp2(state[0] - m_new) * state[1] + pv

    pending = [scores(*items[n]) for n in range(min(ATTN_LOOKAHEAD, len(items)))]
    states, outs = {}, {}
    for n, (i, h, c) in enumerate(items):
        if n + ATTN_LOOKAHEAD < len(items):
            pending.append(scores(*items[n + ATTN_LOOKAHEAD]))
        states[h] = update(states.get(h), pending.pop(0), h, c)
        if c == i:
            acc = states.pop(h)[1]
            outs[h] = acc[:V_DIM] * (1.0 / acc[V_DIM:V_DIM + 1])
            if h % 2 == 1:
                ot = jnp.concatenate([outs[h - 1], outs[h]], axis=0)
                o_ref[i * tq:(i + 1) * tq, (h - 1) * V_DIM:(h + 1) * V_DIM] = ot.T.astype(o_ref.dtype)


def _attention(qt, k, vt, km, vmt, tq):
    b, s, _ = k.shape
    g = ATTN_HEADS
    return pl.pallas_call(
        functools.partial(_attn_kernel, tq=tq),
        grid=(b, N_HEADS // g),
        in_specs=[
            pl.BlockSpec((None, g * HEAD_W, s), lambda bi, hp: (bi, hp, 0)),
            pl.BlockSpec((None, s, g * HEAD_W), lambda bi, hp: (bi, 0, hp)),
            pl.BlockSpec((None, g * V_ROWS, s), lambda bi, hp: (bi, hp, 0)),
            pl.BlockSpec((N_META, g * HEAD_W), lambda bi, hp: (0, hp)),
            pl.BlockSpec((g * V_ROWS, N_META), lambda bi, hp: (hp, 0)),
        ],
        out_specs=pl.BlockSpec((None, s, g * V_DIM), lambda bi, hp: (bi, 0, hp)),
        out_shape=jax.ShapeDtypeStruct((b, s, N_HEADS * V_DIM), BF16),
        compiler_params=pltpu.CompilerParams(
            dimension_semantics=("parallel", "parallel"),
            vmem_limit_bytes=VMEM_LIMIT),
        name="attention",
    )(qt, k, vt, km, vmt)


CONV_PAD = 32
CONV_RB = 64
FF_CHUNK = 256
FF_GROUP = 6
PRE_FFN_UNITS = 3
CONV_SPARE = 32


def _tail_kernel(x_ref, o_ref, sga_ref, sgb_ref, u_ref, halo_ref, mhalo_ref,
                 cw_ref, cb_ref, lng_ref, lnb_ref, wao_ref, wco_ref, wout_ref,
                 gpost_ref, gfpre_ref, wfin_ref, wfout_ref, gfpost_ref,
                 out_ref, buf_ref, mix_ref, act_ref, *, tiles_per_seq, n_tiles):
    s = pl.program_id(0)
    tm, d = u_ref.shape
    d_ff = wfout_ref.shape[0]

    @pl.when(s == 0)
    def _():
        mix_ref[...] = jnp.zeros(mix_ref.shape, mix_ref.dtype)

    conv_tile = jnp.minimum(s, n_tiles - 1)
    first_in_seq = conv_tile % tiles_per_seq == 0
    row0 = jnp.minimum(s, 0)

    hist = jnp.where(first_in_seq, mhalo_ref[...], halo_ref[...]).astype(F32)
    for cb in range(d // LANES):
        csl = slice(cb * LANES, (cb + 1) * LANES)
        buf_ref[cb, 0:CONV_PAD, :] = hist[:, csl]
        buf_ref[cb, CONV_PAD:CONV_PAD + tm, :] = u_ref[:, csl].astype(F32)

    base = CONV_PAD - (CONV_K - 1)
    pieces = [(cb, r0) for cb in range(d // LANES) for r0 in range(0, tm, CONV_RB)]
    n_pieces = len(pieces)
    n_chunks = d_ff // FF_CHUNK
    n_groups = -(-n_chunks // FF_GROUP)
    wide, narrow = d // FF_CHUNK, 1
    budget = PRE_FFN_UNITS + 2 * (n_chunks - 1) * narrow + (n_groups - 1) * wide
    spent = [0]
    yc_c0, hb_c0 = 0, d
    gate_row = pl.multiple_of(row0, 16)

    def gated(c0):
        return mix_ref[pl.ds(gate_row, tm), c0:c0 + d]

    def conv_pieces(cost):
        lo = min(spent[0], budget) * n_pieces // budget
        spent[0] += cost
        hi = min(spent[0], budget) * n_pieces // budget
        for _ in range(hi - lo):
            cb, r0 = pieces.pop(0)
            csl = slice(cb * LANES, (cb + 1) * LANES)
            acc = jnp.broadcast_to(cb_ref[:, csl], (CONV_RB, LANES))
            for res in range(8):
                taps = list(range(res, CONV_K, 8))
                n_rows = CONV_RB + 8 * (len(taps) - 1)
                win = buf_ref[cb, pl.ds(row0 + (r0 + base + res), n_rows), :]
                for t, kk in enumerate(taps):
                    acc = acc + cw_ref[kk:kk + 1, csl] * win[8 * t:8 * t + CONV_RB]
            mix_ref[r0:r0 + CONV_RB, yc_c0 + cb * LANES:yc_c0 + (cb + 1) * LANES] = acc.astype(BF16)

    yc = mix_ref[:, yc_c0:yc_c0 + d].astype(F32)
    mu = jnp.mean(yc, axis=-1, keepdims=True)
    xc = yc - mu
    var = jnp.mean(xc * xc, axis=-1, keepdims=True)
    z = xc * lax.rsqrt(var + EPS) * lng_ref[...] + lnb_ref[...]
    z = (z * _sigmoid(z)).astype(BF16)

    y_a = _dot(o_ref[...], wao_ref[...])
    y_b = _dot(z, wco_ref[...])
    merged = sga_ref[...].astype(F32) * y_a + sgb_ref[...].astype(F32) * y_b
    mo = _dot(merged.astype(BF16), wout_ref[...])
    buf_ref[0, CONV_PAD + tm:, :] = y_b[-CONV_SPARE:, -LANES:]
    conv_pieces(PRE_FFN_UNITS)
    x1 = x_ref[...] + _rms(mo, gpost_ref[...])

    mix_ref[:, hb_c0:hb_c0 + d] = _rms(x1, gfpre_ref[...]).astype(BF16)
    f = None
    for grp in range(n_groups):
        g0 = grp * FF_GROUP * FF_CHUNK
        g1 = min(g0 + FF_GROUP * FF_CHUNK, d_ff)
        for c0 in range(g0, g1, FF_CHUNK):
            if (c0 // FF_CHUNK) % 2 == 0 or c0 == d_ff - FF_CHUNK:
                hb = gated(hb_c0)
            g = _dot(hb, wfin_ref[:, c0:c0 + FF_CHUNK])
            conv_pieces(narrow)
            up = _dot(hb, wfin_ref[:, d_ff + c0:d_ff + c0 + FF_CHUNK])
            conv_pieces(narrow)
            act_ref[:, c0:c0 + FF_CHUNK] = ((g + g * jnp.tanh(g)) * up).astype(BF16)
        part = _dot(act_ref[:, g0:g1], wfout_ref[g0:g1, :])
        conv_pieces(wide)
        f = part if f is None else f + part
    assert not pieces
    out_ref[...] = x1 + _rms(f, gfpost_ref[...])


def _tail(x2d, o2d, sga, sgb, u2d, mhalo, weights, tm, seq):
    rows, d = x2d.shape
    d_ff = weights[10].shape[0]
    n_tiles = rows // tm
    halo_per_tile = tm // CONV_PAD
    lag_spec = pl.BlockSpec((tm, d), lambda s: (jnp.maximum(s - 1, 0), 0))
    conv_tile = lambda s: jnp.minimum(s, n_tiles - 1)
    return pl.pallas_call(
        functools.partial(_tail_kernel, tiles_per_seq=seq // tm, n_tiles=n_tiles),
        grid=(n_tiles + 1,),
        in_specs=[lag_spec] * 4 + [
            pl.BlockSpec((tm, d), lambda s: (conv_tile(s), 0)),
            pl.BlockSpec((CONV_PAD, d),
                         lambda s: (jnp.maximum(conv_tile(s) * halo_per_tile - 1, 0), 0)),
        ] + [_const_spec(mhalo.shape)] + [_const_spec(w.shape) for w in weights],
        out_specs=lag_spec,
        out_shape=jax.ShapeDtypeStruct((rows, d), F32),
        scratch_shapes=[pltpu.VMEM((d // LANES, tm + CONV_PAD + CONV_SPARE, LANES), F32),
                        pltpu.VMEM((tm, 2 * d), BF16),
                        pltpu.VMEM((tm, d_ff), BF16)],
        compiler_params=pltpu.CompilerParams(
            dimension_semantics=("arbitrary",), vmem_limit_bytes=VMEM_LIMIT),
        name="tail",
    )(x2d, o2d, sga, sgb, u2d, u2d, mhalo, *weights)


def _rope_tables(length):
    pos = np.arange(length, dtype=np.float32)
    inv = np.float32(ROPE_BASE) ** (-np.arange(0, QK_ROPE, 2, dtype=np.float32) / QK_ROPE)
    ang = (pos[:, None] * inv[None, :]).astype(np.float32)
    cos, sin = np.cos(ang), np.sin(ang)
    zeros = lambda w: np.zeros((length, w), np.float32)
    ktail = LANES - QK_ROPE
    ck = np.concatenate([cos, cos, zeros(ktail)], axis=1)
    sk_lo = np.concatenate([-sin, zeros(HALF_ROPE + ktail)], axis=1)
    sk_hi = np.concatenate([zeros(HALF_ROPE), sin, zeros(ktail)], axis=1)
    tabs = np.stack([ck, sk_lo, sk_hi]).astype(np.float32)
    tabs_t = (np.stack([cos.T, sin.T]) * np.float32(Q_SCALE)).astype(np.float32)
    return tabs, tabs_t


def kernel(x, meta, mix_pre_g, w_in, q_norm_g, w_uq, kv_norm_g, w_ukv, w_attn_o,
           conv_w, conv_b, conv_ln_g, conv_ln_b, w_conv_o, w_out, mix_post_g,
           ffn_pre_g, w_ffn_in, w_ffn_out, ffn_post_g):
    assert w_in.shape[0] == 1, "single-layer block"
    b, s, d = x.shape
    row = lambda g: g[0][None, :].astype(F32)

    w_in0 = w_in[0]
    n_in = w_in0.shape[1]
    col_scale = jnp.concatenate([jnp.ones((n_in - 3 * d,), F32), jnp.full((3 * d,), 0.5, F32)])
    wint = (w_in0.T * col_scale[:, None]).astype(BF16)

    qk_dim = QK_NOPE + QK_ROPE
    wq = w_uq[0].reshape(Q_RANK, N_HEADS, qk_dim)
    wq = jnp.pad(wq, ((0, 0), (0, 0), (0, HEAD_W - qk_dim)))
    wqt = wq.reshape(Q_RANK, N_HEADS * HEAD_W).T.astype(BF16)

    wkv = w_ukv[0].reshape(KV_RANK, N_HEADS, QK_NOPE + V_DIM)
    wk_nope = jnp.pad(wkv[:, :, :QK_NOPE], ((0, 0), (0, 0), (0, HEAD_W - QK_NOPE)))
    place = np.zeros((LANES, N_HEADS, HEAD_W), np.float32)
    jj = np.arange(QK_ROPE)
    place[jj, :, QK_NOPE + jj] = 1.0
    wk = jnp.concatenate([wk_nope, jnp.asarray(place)], axis=0)
    wk = wk.reshape(KV_RANK + LANES, N_HEADS * HEAD_W).astype(BF16)
    wvt = wkv[:, :, QK_NOPE:].reshape(KV_RANK, N_HEADS * V_DIM).T.astype(BF16)

    in_weights = (row(mix_pre_g), wint, row(q_norm_g), wqt, row(kv_norm_g), wk, wvt)

    tabs, tabs_t = _rope_tables(N_META + s)

    tm = 256
    x2d = x.reshape(b * s, d)
    qt, k, vt, u, sga, sgb = _inproj(x2d, tabs[:, N_META:], tabs_t[:, :, N_META:],
                                     in_weights, 2 * tm, s)
    _, km, vmt, um, _, _ = _inproj(meta.astype(F32), tabs[:, :N_META],
                                   tabs_t[:, :, :N_META], in_weights, N_META, N_META)

    o = _attention(qt, k.reshape(b, s, -1), vt, km, vmt[0], tq=256)

    mhalo = jnp.concatenate([jnp.zeros((CONV_PAD - N_META, d), BF16), um], axis=0)
    d_ff = w_ffn_out.shape[1]
    ffn_scale = jnp.concatenate([jnp.full((d_ff,), 0.5, F32), jnp.ones((d_ff,), F32)])
    tail_weights = (conv_w[0].astype(F32), row(conv_b), row(conv_ln_g), row(conv_ln_b),
                    w_attn_o[0].astype(BF16), w_conv_o[0].astype(BF16),
                    w_out[0].astype(BF16), row(mix_post_g), row(ffn_pre_g),
                    (w_ffn_in[0] * ffn_scale).astype(BF16), w_ffn_out[0].astype(BF16),
                    row(ffn_post_g))
    out = _tail(x2d, o.reshape(b * s, -1), sga, sgb, u, mhalo, tail_weights, tm, s)
    return out.reshape(b, s, d)
```

```python
import functools
import math

import jax
import jax.numpy as jnp
import numpy as np
from jax import lax
from jax.experimental import pallas as pl
from jax.experimental.pallas import tpu as pltpu

CHUNK = 64
N_META = 16
N_HEADS = 16
QK_NOPE = 64
QK_ROPE = 32
V_DIM = 64
Q_RANK = 256
KV_RANK = 128
ROPE_BASE = 10000.0
CONV_K = 31
EPS = 1e-6
NEG_INF = -1e30

LANES = 128
VMEM_LIMIT = 56 * 1024 * 1024

HEAD_W = LANES
HALF_ROPE = QK_ROPE // 2
V_ROWS = V_DIM + 16
Q_SCALE = math.log2(math.e) / math.sqrt(QK_NOPE + QK_ROPE)

BF16 = jnp.bfloat16
F32 = jnp.float32


def _rms(x, g):
    ms = jnp.mean(x * x, axis=-1, keepdims=True)
    return x * lax.rsqrt(ms + EPS) * g


def _sigmoid(x):
    return 0.5 * (jnp.tanh(0.5 * x) + 1.0)


def _sigmoid_of_twice(h):
    return 0.5 * jnp.tanh(h) + 0.5


def _dot(a, b):
    return jnp.dot(a, b, preferred_element_type=F32)


def _dot_t(a, b):
    return lax.dot_general(a, b, (((1,), (1,)), ((), ())),
                           preferred_element_type=F32)


def _rope(g, c, s_lo, s_hi):
    up = pltpu.roll(g, LANES - HALF_ROPE, 1)
    dn = pltpu.roll(g, HALF_ROPE, 1)
    return g * c + up * s_lo + dn * s_hi


def _inproj_kernel(x_ref, tab_ref, tabt_ref, gpre_ref, wint_ref, gq_ref,
                   wqt_ref, gkv_ref, wk_ref, wvt_ref,
                   qt_ref, k_ref, vt_ref, u_ref, sga_ref, sgb_ref):
    d = x_ref.shape[1]
    xb = _rms(x_ref[...], gpre_ref[...]).astype(BF16)

    off = Q_RANK + KV_RANK + QK_ROPE
    small = _dot_t(xb, wint_ref[0:4 * LANES, :])
    a = _dot_t(xb, wint_ref[off:off + d, :])
    g = _dot_t(xb, wint_ref[off + d:off + 2 * d, :])
    u_ref[...] = (a * _sigmoid_of_twice(g)).astype(BF16)
    qn = _rms(small[:, :Q_RANK], gq_ref[...]).astype(BF16)
    kvn = _rms(small[:, Q_RANK:Q_RANK + KV_RANK], gkv_ref[...]).astype(BF16)
    kr = _rope(small[:, Q_RANK + KV_RANK:], tab_ref[0], tab_ref[1], tab_ref[2])

    kin = jnp.concatenate([kvn, kr.astype(BF16)], axis=1)
    k_ref[...] = _dot(kin, wk_ref[...]).astype(BF16)
    vt = _dot_t(wvt_ref[...], kvn).astype(BF16)
    for h in range(N_HEADS):
        vt_ref[h * V_ROWS:h * V_ROWS + V_DIM, :] = vt[h * V_DIM:(h + 1) * V_DIM]
        vt_ref[h * V_ROWS + V_DIM:(h + 1) * V_ROWS, :] = jnp.ones(
            (V_ROWS - V_DIM, vt.shape[1]), BF16)

    qt = _dot_t(wqt_ref[...], qn)
    cos_t, sin_t = tabt_ref[0], tabt_ref[1]
    scale = Q_SCALE
    for h in range(N_HEADS):
        r0 = h * HEAD_W
        lo = qt[r0 + QK_NOPE:r0 + QK_NOPE + HALF_ROPE]
        hi = qt[r0 + QK_NOPE + HALF_ROPE:r0 + QK_NOPE + QK_ROPE]
        qt_ref[r0:r0 + QK_NOPE, :] = (qt[r0:r0 + QK_NOPE] * scale).astype(BF16)
        qt_ref[r0 + QK_NOPE:r0 + QK_NOPE + QK_ROPE, :] = jnp.concatenate(
            [lo * cos_t - hi * sin_t, hi * cos_t + lo * sin_t], axis=0).astype(BF16)
        qt_ref[r0 + QK_NOPE + QK_ROPE:r0 + HEAD_W, :] = jnp.zeros(
            (HEAD_W - QK_NOPE - QK_ROPE, qt.shape[1]), BF16)

    sga_ref[...] = _sigmoid_of_twice(
        _dot_t(xb, wint_ref[off + 2 * d:off + 3 * d, :])).astype(BF16)
    sgb_ref[...] = _sigmoid_of_twice(
        _dot_t(xb, wint_ref[off + 3 * d:off + 4 * d, :])).astype(BF16)


def _const_spec(shape):
    nd = len(shape)
    return pl.BlockSpec(shape, lambda *_: (0,) * nd, pipeline_mode=pl.Buffered(1))


def _inproj(x2d, tabs, tabs_t, weights, tm, seq):
    rows, d = x2d.shape
    gpre, wint, gq, wqt, gkv, wk, wvt = weights
    n_q = wqt.shape[0]
    n_v = N_HEADS * V_ROWS
    tps = seq // tm
    row_spec = lambda w: pl.BlockSpec((tm, w), lambda i: (i, 0))
    col_spec = lambda h: pl.BlockSpec((None, h, tm), lambda i: (i // tps, 0, i % tps))
    out_shapes = [
        jax.ShapeDtypeStruct((rows // seq, n_q, seq), BF16),
        jax.ShapeDtypeStruct((rows, n_q), BF16),
        jax.ShapeDtypeStruct((rows // seq, n_v, seq), BF16),
        jax.ShapeDtypeStruct((rows, d), BF16),
        jax.ShapeDtypeStruct((rows, d), BF16),
        jax.ShapeDtypeStruct((rows, d), BF16),
    ]
    return pl.pallas_call(
        _inproj_kernel,
        grid=(rows // tm,),
        in_specs=[
            row_spec(d),
            pl.BlockSpec((3, tm, LANES), lambda i: (0, i % tps, 0)),
            pl.BlockSpec((2, HALF_ROPE, tm), lambda i: (0, 0, i % tps)),
            _const_spec(gpre.shape), _const_spec(wint.shape), _const_spec(gq.shape),
            _const_spec(wqt.shape), _const_spec(gkv.shape),
            _const_spec(wk.shape), _const_spec(wvt.shape),
        ],
        out_specs=[col_spec(n_q), row_spec(n_q), col_spec(n_v),
                   row_spec(d), row_spec(d), row_spec(d)],
        out_shape=out_shapes,
        compiler_params=pltpu.CompilerParams(
            dimension_semantics=("parallel",), vmem_limit_bytes=VMEM_LIMIT),
        name="inproj",
    )(x2d, tabs, tabs_t, gpre, wint, gq, wqt, gkv, wk, wvt)


ATTN_LOOKAHEAD = 6
ATTN_HEADS = 4


def _attn_kernel(qt_ref, k_ref, vt_ref, km_ref, vmt_ref, o_ref, *, tq):
    s_len = k_ref.shape[0]
    key_chunk = lax.broadcasted_iota(jnp.int32, (tq, tq), 0) // CHUNK
    qry_chunk = lax.broadcasted_iota(jnp.int32, (tq, tq), 1) // CHUNK
    diag_ok = key_chunk <= qry_chunk

    items = []
    for i in reversed(range(s_len // tq)):
        for h in range(ATTN_HEADS):
            items += [(i, h, c) for c in [None] + list(range(i + 1))]

    def scores(i, h, c):
        ksl = slice(h * HEAD_W, (h + 1) * HEAD_W)
        qt = qt_ref[ksl, i * tq:(i + 1) * tq]
        if c is None:
            return _dot(km_ref[:, ksl], qt)
        s = _dot(k_ref[c * tq:(c + 1) * tq, ksl], qt)
        return jnp.where(diag_ok, s, NEG_INF) if c == i else s

    def update(state, s, h, c):
        vsl = slice(h * V_ROWS, (h + 1) * V_ROWS)
        cm = jnp.max(s, axis=0, keepdims=True)
        m_new = cm if state is None else jnp.maximum(state[0], cm)
        p = jnp.exp2(s - m_new).astype(BF16)
        v = vmt_ref[vsl, :] if c is None else vt_ref[vsl, c * tq:(c + 1) * tq]
        pv = _dot(v, p)
        if state is None:
            return m_new, pv
        return m_new, jnp.exp2(state[0] - m_new) * state[1] + pv

    pending = [scores(*items[n]) for n in range(min(ATTN_LOOKAHEAD, len(items)))]
    states, outs = {}, {}
    for n, (i, h, c) in enumerate(items):
        if n + ATTN_LOOKAHEAD < len(items):
            pending.append(scores(*items[n + ATTN_LOOKAHEAD]))
        states[h] = update(states.get(h), pending.pop(0), h, c)
        if c == i:
            acc = states.pop(h)[1]
            outs[h] = acc[:V_DIM] * (1.0 / acc[V_DIM:V_DIM + 1])
            if h % 2 == 1:
                ot = jnp.concatenate([outs[h - 1], outs[h]], axis=0)
                o_ref[i * tq:(i + 1) * tq, (h - 1) * V_DIM:(h + 1) * V_DIM] = ot.T.astype(o_ref.dtype)


def _attention(qt, k, vt, km, vmt, tq):
    b, s, _ = k.shape
    g = ATTN_HEADS
    return pl.pallas_call(
        functools.partial(_attn_kernel, tq=tq),
        grid=(b, N_HEADS // g),
        in_specs=[
            pl.BlockSpec((None, g * HEAD_W, s), lambda bi, hp: (bi, hp, 0)),
            pl.BlockSpec((None, s, g * HEAD_W), lambda bi, hp: (bi, 0, hp)),
            pl.BlockSpec((None, g * V_ROWS, s), lambda bi, hp: (bi, hp, 0)),
            pl.BlockSpec((N_META, g * HEAD_W), lambda bi, hp: (0, hp)),
            pl.BlockSpec((g * V_ROWS, N_META), lambda bi, hp: (hp, 0)),
        ],
        out_specs=pl.BlockSpec((None, s, g * V_DIM), lambda bi, hp: (bi, 0, hp)),
        out_shape=jax.ShapeDtypeStruct((b, s, N_HEADS * V_DIM), BF16),
        compiler_params=pltpu.CompilerParams(
            dimension_semantics=("parallel", "parallel"),
            vmem_limit_bytes=VMEM_LIMIT),
        name="attention",
    )(qt, k, vt, km, vmt)


CONV_PAD = 32
CONV_RB = 64
FF_CHUNK = 256
FF_GROUP = 11
PRE_FFN_UNITS = 3
CONV_SPARE = 8


def _tail_kernel(x_ref, o_ref, sga_ref, sgb_ref, u_ref, halo_ref, mhalo_ref,
                 cw_ref, cb_ref, lng_ref, lnb_ref, wao_ref, wco_ref, wout_ref,
                 gpost_ref, gfpre_ref, wfin_ref, wfout_ref, gfpost_ref,
                 out_ref, buf_ref, mix_ref, act_ref, *, tiles_per_seq, n_tiles):
    s = pl.program_id(0)
    tm, d = u_ref.shape
    d_ff = wfout_ref.shape[0]

    @pl.when(s == 0)
    def _():
        mix_ref[...] = jnp.zeros(mix_ref.shape, mix_ref.dtype)

    conv_tile = jnp.minimum(s, n_tiles - 1)
    first_in_seq = conv_tile % tiles_per_seq == 0
    row0 = jnp.minimum(s, 0)

    hist = jnp.where(first_in_seq, mhalo_ref[...], halo_ref[...]).astype(F32)
    for cb in range(d // LANES):
        csl = slice(cb * LANES, (cb + 1) * LANES)
        buf_ref[cb, 0:CONV_PAD, :] = hist[:, csl]
        buf_ref[cb, CONV_PAD:CONV_PAD + tm, :] = u_ref[:, csl].astype(F32)

    base = CONV_PAD - (CONV_K - 1)
    pieces = [(cb, r0) for cb in range(d // LANES) for r0 in range(0, tm, CONV_RB)]
    n_pieces = len(pieces)
    n_chunks = d_ff // FF_CHUNK
    n_groups = -(-n_chunks // FF_GROUP)
    wide, narrow = d // FF_CHUNK, 1
    budget = PRE_FFN_UNITS + 2 * (n_chunks - 1) * narrow + (n_groups - 1) * wide
    spent = [0]
    yc_c0, hb_c0 = 0, d
    gate_row = pl.multiple_of(row0, 16)

    def gated(c0):
        return mix_ref[pl.ds(gate_row, tm), c0:c0 + d]

    def conv_pieces(cost):
        lo = min(spent[0], budget) * n_pieces // budget
        spent[0] += cost
        hi = min(spent[0], budget) * n_pieces // budget
        for _ in range(hi - lo):
            cb, r0 = pieces.pop(0)
            csl = slice(cb * LANES, (cb + 1) * LANES)
            acc = jnp.broadcast_to(cb_ref[:, csl], (CONV_RB, LANES))
            for res in range(8):
                taps = list(range(res, CONV_K, 8))
                n_rows = CONV_RB + 8 * (len(taps) - 1)
                win = buf_ref[cb, pl.ds(row0 + (r0 + base + res), n_rows), :]
                for t, kk in enumerate(taps):
                    acc = acc + cw_ref[kk:kk + 1, csl] * win[8 * t:8 * t + CONV_RB]
            mix_ref[r0:r0 + CONV_RB, yc_c0 + cb * LANES:yc_c0 + (cb + 1) * LANES] = acc.astype(BF16)

    yc = mix_ref[:, yc_c0:yc_c0 + d].astype(F32)
    mu = jnp.mean(yc, axis=-1, keepdims=True)
    xc = yc - mu
    var = jnp.mean(xc * xc, axis=-1, keepdims=True)
    z = xc * lax.rsqrt(var + EPS) * lng_ref[...] + lnb_ref[...]
    z = (z * _sigmoid(z)).astype(BF16)

    y_a = _dot(o_ref[...], wao_ref[...])
    y_b = _dot(z, wco_ref[...])
    merged = sga_ref[...].astype(F32) * y_a + sgb_ref[...].astype(F32) * y_b
    mo = _dot(merged.astype(BF16), wout_ref[...])
    buf_ref[0, CONV_PAD + tm:, :] = y_b[-CONV_SPARE:, -LANES:]
    conv_pieces(PRE_FFN_UNITS)
    x1 = x_ref[...] + _rms(mo, gpost_ref[...])

    mix_ref[:, hb_c0:hb_c0 + d] = _rms(x1, gfpre_ref[...]).astype(BF16)
    f = None
    for grp in range(n_groups):
        g0 = grp * FF_GROUP * FF_CHUNK
        g1 = min(g0 + FF_GROUP * FF_CHUNK, d_ff)
        for c0 in range(g0, g1, FF_CHUNK):
            if (c0 // FF_CHUNK) % 2 == 0 or c0 == d_ff - FF_CHUNK:
                hb = gated(hb_c0)
            g = _dot(hb, wfin_ref[:, c0:c0 + FF_CHUNK])
            conv_pieces(narrow)
            up = _dot(hb, wfin_ref[:, d_ff + c0:d_ff + c0 + FF_CHUNK])
            conv_pieces(narrow)
            act_ref[:, c0:c0 + FF_CHUNK] = ((g + g * jnp.tanh(g)) * up).astype(BF16)
        part = _dot(act_ref[:, g0:g1], wfout_ref[g0:g1, :])
        conv_pieces(wide)
        f = part if f is None else f + part
    assert not pieces
    out_ref[...] = x1 + _rms(f, gfpost_ref[...])


def _tail(x2d, o2d, sga, sgb, u2d, mhalo, weights, tm, seq):
    rows, d = x2d.shape
    d_ff = weights[10].shape[0]
    n_tiles = rows // tm
    halo_per_tile = tm // CONV_PAD
    lag_spec = pl.BlockSpec((tm, d), lambda s: (jnp.maximum(s - 1, 0), 0))
    conv_tile = lambda s: jnp.minimum(s, n_tiles - 1)
    return pl.pallas_call(
        functools.partial(_tail_kernel, tiles_per_seq=seq // tm, n_tiles=n_tiles),
        grid=(n_tiles + 1,),
        in_specs=[lag_spec] * 4 + [
            pl.BlockSpec((tm, d), lambda s: (conv_tile(s), 0)),
            pl.BlockSpec((CONV_PAD, d),
                         lambda s: (jnp.maximum(conv_tile(s) * halo_per_tile - 1, 0), 0)),
        ] + [_const_spec(mhalo.shape)] + [_const_spec(w.shape) for w in weights],
        out_specs=lag_spec,
        out_shape=jax.ShapeDtypeStruct((rows, d), F32),
        scratch_shapes=[pltpu.VMEM((d // LANES, tm + CONV_PAD + CONV_SPARE, LANES), F32),
                        pltpu.VMEM((tm, 2 * d), BF16),
                        pltpu.VMEM((tm, d_ff), BF16)],
        compiler_params=pltpu.CompilerParams(
            dimension_semantics=("arbitrary",), vmem_limit_bytes=VMEM_LIMIT),
        name="tail",
    )(x2d, o2d, sga, sgb, u2d, u2d, mhalo, *weights)


def _rope_tables(length):
    pos = np.arange(length, dtype=np.float32)
    inv = np.float32(ROPE_BASE) ** (-np.arange(0, QK_ROPE, 2, dtype=np.float32) / QK_ROPE)
    ang = (pos[:, None] * inv[None, :]).astype(np.float32)
    cos, sin = np.cos(ang), np.sin(ang)
    zeros = lambda w: np.zeros((length, w), np.float32)
    ktail = LANES - QK_ROPE
    ck = np.concatenate([cos, cos, zeros(ktail)], axis=1)
    sk_lo = np.concatenate([-sin, zeros(HALF_ROPE + ktail)], axis=1)
    sk_hi = np.concatenate([zeros(HALF_ROPE), sin, zeros(ktail)], axis=1)
    tabs = np.stack([ck, sk_lo, sk_hi]).astype(np.float32)
    tabs_t = (np.stack([cos.T, sin.T]) * np.float32(Q_SCALE)).astype(np.float32)
    return tabs, tabs_t


def kernel(x, meta, mix_pre_g, w_in, q_norm_g, w_uq, kv_norm_g, w_ukv, w_attn_o,
           conv_w, conv_b, conv_ln_g, conv_ln_b, w_conv_o, w_out, mix_post_g,
           ffn_pre_g, w_ffn_in, w_ffn_out, ffn_post_g):
    assert w_in.shape[0] == 1, "single-layer block"
    b, s, d = x.shape
    row = lambda g: g[0][None, :].astype(F32)

    w_in0 = w_in[0]
    n_in = w_in0.shape[1]
    col_scale = jnp.concatenate([jnp.ones((n_in - 3 * d,), F32), jnp.full((3 * d,), 0.5, F32)])
    wint = (w_in0.T * col_scale[:, None]).astype(BF16)

    qk_dim = QK_NOPE + QK_ROPE
    wq = w_uq[0].reshape(Q_RANK, N_HEADS, qk_dim)
    wq = jnp.pad(wq, ((0, 0), (0, 0), (0, HEAD_W - qk_dim)))
    wqt = wq.reshape(Q_RANK, N_HEADS * HEAD_W).T.astype(BF16)

    wkv = w_ukv[0].reshape(KV_RANK, N_HEADS, QK_NOPE + V_DIM)
    wk_nope = jnp.pad(wkv[:, :, :QK_NOPE], ((0, 0), (0, 0), (0, HEAD_W - QK_NOPE)))
    place = np.zeros((LANES, N_HEADS, HEAD_W), np.float32)
    jj = np.arange(QK_ROPE)
    place[jj, :, QK_NOPE + jj] = 1.0
    wk = jnp.concatenate([wk_nope, jnp.asarray(place)], axis=0)
    wk = wk.reshape(KV_RANK + LANES, N_HEADS * HEAD_W).astype(BF16)
    wvt = wkv[:, :, QK_NOPE:].reshape(KV_RANK, N_HEADS * V_DIM).T.astype(BF16)

    in_weights = (row(mix_pre_g), wint, row(q_norm_g), wqt, row(kv_norm_g), wk, wvt)

    tabs, tabs_t = _rope_tables(N_META + s)

    tm = 256
    x2d = x.reshape(b * s, d)
    qt, k, vt, u, sga, sgb = _inproj(x2d, tabs[:, N_META:], tabs_t[:, :, N_META:],
                                     in_weights, 2 * tm, s)
    _, km, vmt, um, _, _ = _inproj(meta.astype(F32), tabs[:, :N_META],
                                   tabs_t[:, :, :N_META], in_weights, N_META, N_META)

    o = _attention(qt, k.reshape(b, s, -1), vt, km, vmt[0], tq=256)

    mhalo = jnp.concatenate([jnp.zeros((CONV_PAD - N_META, d), BF16), um], axis=0)
    d_ff = w_ffn_out.shape[1]
    ffn_scale = jnp.concatenate([jnp.full((d_ff,), 0.5, F32), jnp.ones((d_ff,), F32)])
    tail_weights = (conv_w[0].astype(F32), row(conv_b), row(conv_ln_g), row(conv_ln_b),
                    w_attn_o[0].astype(BF16), w_conv_o[0].astype(BF16),
                    w_out[0].astype(BF16), row(mix_post_g), row(ffn_pre_g),
                    (w_ffn_in[0] * ffn_scale).astype(BF16), w_ffn_out[0].astype(BF16),
                    row(ffn_post_g))
    out = _tail(x2d, o.reshape(b * s, -1), sga, sgb, u, mhalo, tail_weights, tm, s)
    return out.reshape(b, s, d)
```

```python
import functools
import math

import jax
import jax.numpy as jnp
import numpy as np
from jax import lax
from jax.experimental import pallas as pl
from jax.experimental.pallas import tpu as pltpu

CHUNK = 64
N_META = 16
N_HEADS = 16
QK_NOPE = 64
QK_ROPE = 32
V_DIM = 64
Q_RANK = 256
KV_RANK = 128
ROPE_BASE = 10000.0
CONV_K = 31
EPS = 1e-6
NEG_INF = -1e30

LANES = 128
VMEM_LIMIT = 56 * 1024 * 1024

HEAD_W = LANES
HALF_ROPE = QK_ROPE // 2
V_ROWS = V_DIM + 16
Q_SCALE = math.log2(math.e) / math.sqrt(QK_NOPE + QK_ROPE)

BF16 = jnp.bfloat16
F32 = jnp.float32


def _rms(x, g):
    ms = jnp.mean(x * x, axis=-1, keepdims=True)
    return x * lax.rsqrt(ms + EPS) * g


def _sigmoid(x):
    return 0.5 * (jnp.tanh(0.5 * x) + 1.0)


def _sigmoid_of_twice(h):
    return 0.5 * jnp.tanh(h) + 0.5


def _dot(a, b):
    return jnp.dot(a, b, preferred_element_type=F32)


def _dot_t(a, b):
    return lax.dot_general(a, b, (((1,), (1,)), ((), ())),
                           preferred_element_type=F32)


def _rope(g, c, s_lo, s_hi):
    up = pltpu.roll(g, LANES - HALF_ROPE, 1)
    dn = pltpu.roll(g, HALF_ROPE, 1)
    return g * c + up * s_lo + dn * s_hi


def _inproj_kernel(x_ref, tab_ref, tabt_ref, gpre_ref, wint_ref, gq_ref,
                   wqt_ref, gkv_ref, wk_ref, wvt_ref,
                   qt_ref, k_ref, vt_ref, u_ref, sga_ref, sgb_ref):
    d = x_ref.shape[1]
    xb = _rms(x_ref[...], gpre_ref[...]).astype(BF16)

    off = Q_RANK + KV_RANK + QK_ROPE
    small = _dot_t(xb, wint_ref[0:4 * LANES, :])
    a = _dot_t(xb, wint_ref[off:off + d, :])
    g = _dot_t(xb, wint_ref[off + d:off + 2 * d, :])
    u_ref[...] = (a * _sigmoid_of_twice(g)).astype(BF16)
    qn = _rms(small[:, :Q_RANK], gq_ref[...]).astype(BF16)
    kvn = _rms(small[:, Q_RANK:Q_RANK + KV_RANK], gkv_ref[...]).astype(BF16)
    kr = _rope(small[:, Q_RANK + KV_RANK:], tab_ref[0], tab_ref[1], tab_ref[2])

    kin = jnp.concatenate([kvn, kr.astype(BF16)], axis=1)
    k_ref[...] = _dot(kin, wk_ref[...]).astype(BF16)
    vt = _dot_t(wvt_ref[...], kvn).astype(BF16)
    for h in range(N_HEADS):
        vt_ref[h * V_ROWS:h * V_ROWS + V_DIM, :] = vt[h * V_DIM:(h + 1) * V_DIM]
        vt_ref[h * V_ROWS + V_DIM:(h + 1) * V_ROWS, :] = jnp.ones(
            (V_ROWS - V_DIM, vt.shape[1]), BF16)

    qt = _dot_t(wqt_ref[...], qn)
    cos_t, sin_t = tabt_ref[0], tabt_ref[1]
    scale = Q_SCALE
    for h in range(N_HEADS):
        r0 = h * HEAD_W
        lo = qt[r0 + QK_NOPE:r0 + QK_NOPE + HALF_ROPE]
        hi = qt[r0 + QK_NOPE + HALF_ROPE:r0 + QK_NOPE + QK_ROPE]
        qt_ref[r0:r0 + QK_NOPE, :] = (qt[r0:r0 + QK_NOPE] * scale).astype(BF16)
        qt_ref[r0 + QK_NOPE:r0 + QK_NOPE + QK_ROPE, :] = jnp.concatenate(
            [lo * cos_t - hi * sin_t, hi * cos_t + lo * sin_t], axis=0).astype(BF16)
        qt_ref[r0 + QK_NOPE + QK_ROPE:r0 + HEAD_W, :] = jnp.zeros(
            (HEAD_W - QK_NOPE - QK_ROPE, qt.shape[1]), BF16)

    sga_ref[...] = _sigmoid_of_twice(
        _dot_t(xb, wint_ref[off + 2 * d:off + 3 * d, :])).astype(BF16)
    sgb_ref[...] = _sigmoid_of_twice(
        _dot_t(xb, wint_ref[off + 3 * d:off + 4 * d, :])).astype(BF16)


def _const_spec(shape):
    nd = len(shape)
    return pl.BlockSpec(shape, lambda *_: (0,) * nd, pipeline_mode=pl.Buffered(1))


def _inproj(x2d, tabs, tabs_t, weights, tm, seq):
    rows, d = x2d.shape
    gpre, wint, gq, wqt, gkv, wk, wvt = weights
    n_q = wqt.shape[0]
    n_v = N_HEADS * V_ROWS
    tps = seq // tm
    row_spec = lambda w: pl.BlockSpec((tm, w), lambda i: (i, 0))
    col_spec = lambda h: pl.BlockSpec((None, h, tm), lambda i: (i // tps, 0, i % tps))
    out_shapes = [
        jax.ShapeDtypeStruct((rows // seq, n_q, seq), BF16),
        jax.ShapeDtypeStruct((rows, n_q), BF16),
        jax.ShapeDtypeStruct((rows // seq, n_v, seq), BF16),
        jax.ShapeDtypeStruct((rows, d), BF16),
        jax.ShapeDtypeStruct((rows, d), BF16),
        jax.ShapeDtypeStruct((rows, d), BF16),
    ]
    return pl.pallas_call(
        _inproj_kernel,
        grid=(rows // tm,),
        in_specs=[
            row_spec(d),
            pl.BlockSpec((3, tm, LANES), lambda i: (0, i % tps, 0)),
            pl.BlockSpec((2, HALF_ROPE, tm), lambda i: (0, 0, i % tps)),
            _const_spec(gpre.shape), _const_spec(wint.shape), _const_spec(gq.shape),
            _const_spec(wqt.shape), _const_spec(gkv.shape),
            _const_spec(wk.shape), _const_spec(wvt.shape),
        ],
        out_specs=[col_spec(n_q), row_spec(n_q), col_spec(n_v),
                   row_spec(d), row_spec(d), row_spec(d)],
        out_shape=out_shapes,
        compiler_params=pltpu.CompilerParams(
            dimension_semantics=("parallel",), vmem_limit_bytes=VMEM_LIMIT),
        name="inproj",
    )(x2d, tabs, tabs_t, gpre, wint, gq, wqt, gkv, wk, wvt)


ATTN_LOOKAHEAD = 6
ATTN_HEADS = 4


def _attn_kernel(qt_ref, k_ref, vt_ref, km_ref, vmt_ref, o_ref, *, tq):
    s_len = k_ref.shape[0]
    key_chunk = lax.broadcasted_iota(jnp.int32, (tq, tq), 0) // CHUNK
    qry_chunk = lax.broadcasted_iota(jnp.int32, (tq, tq), 1) // CHUNK
    diag_ok = key_chunk <= qry_chunk

    items = []
    for i in reversed(range(s_len // tq)):
        for h in range(ATTN_HEADS):
            items += [(i, h, c) for c in [None] + list(range(i + 1))]

    def scores(i, h, c):
        ksl = slice(h * HEAD_W, (h + 1) * HEAD_W)
        qt = qt_ref[ksl, i * tq:(i + 1) * tq]
        if c is None:
            return _dot(km_ref[:, ksl], qt)
        s = _dot(k_ref[c * tq:(c + 1) * tq, ksl], qt)
        return jnp.where(diag_ok, s, NEG_INF) if c == i else s

    def update(state, s, h, c):
        vsl = slice(h * V_ROWS, (h + 1) * V_ROWS)
        cm = jnp.max(s, axis=0, keepdims=True)
        m_new = cm if state is None else jnp.maximum(state[0], cm)
        p = jnp.exp2(s - m_new).astype(BF16)
        v = vmt_ref[vsl, :] if c is None else vt_ref[vsl, c * tq:(c + 1) * tq]
        pv = _dot(v, p)
        if state is None:
            return m_new, pv
        return m_new, jnp.exp2(state[0] - m_new) * state[1] + pv

    pending = [scores(*items[n]) for n in range(min(ATTN_LOOKAHEAD, len(items)))]
    states, outs = {}, {}
    for n, (i, h, c) in enumerate(items):
        if n + ATTN_LOOKAHEAD < len(items):
            pending.append(scores(*items[n + ATTN_LOOKAHEAD]))
        states[h] = update(states.get(h), pending.pop(0), h, c)
        if c == i:
            acc = states.pop(h)[1]
            outs[h] = acc[:V_DIM] * (1.0 / acc[V_DIM:V_DIM + 1])
            if h % 2 == 1:
                ot = jnp.concatenate([outs[h - 1], outs[h]], axis=0)
                o_ref[i * tq:(i + 1) * tq, (h - 1) * V_DIM:(h + 1) * V_DIM] = ot.T.astype(o_ref.dtype)


def _attention(qt, k, vt, km, vmt, tq):
    b, s, _ = k.shape
    g = ATTN_HEADS
    return pl.pallas_call(
        functools.partial(_attn_kernel, tq=tq),
        grid=(b, N_HEADS // g),
        in_specs=[
            pl.BlockSpec((None, g * HEAD_W, s), lambda bi, hp: (bi, hp, 0)),
            pl.BlockSpec((None, s, g * HEAD_W), lambda bi, hp: (bi, 0, hp)),
            pl.BlockSpec((None, g * V_ROWS, s), lambda bi, hp: (bi, hp, 0)),
            pl.BlockSpec((N_META, g * HEAD_W), lambda bi, hp: (0, hp)),
            pl.BlockSpec((g * V_ROWS, N_META), lambda bi, hp: (hp, 0)),
        ],
        out_specs=pl.BlockSpec((None, s, g * V_DIM), lambda bi, hp: (bi, 0, hp)),
        out_shape=jax.ShapeDtypeStruct((b, s, N_HEADS * V_DIM), BF16),
        compiler_params=pltpu.CompilerParams(
            dimension_semantics=("parallel", "parallel"),
            vmem_limit_bytes=VMEM_LIMIT),
        name="attention",
    )(qt, k, vt, km, vmt)


CONV_PAD = 32
CONV_RB = 64
FF_CHUNK = 256
FF_GROUP = 6
PRE_FFN_UNITS = 3
CONV_SPARE = 8


def _tail_kernel(x_ref, o_ref, sga_ref, sgb_ref, u_ref, halo_ref, mhalo_ref,
                 cw_ref, cb_ref, lng_ref, lnb_ref, wao_ref, wco_ref, wout_ref,
                 gpost_ref, gfpre_ref, wfin_ref, wfout_ref, gfpost_ref,
                 out_ref, buf_ref, mix_ref, act_ref, wao_b, wco_b, wout_b, *,
                 tiles_per_seq, n_tiles):
    s = pl.program_id(0)
    tm, d = u_ref.shape
    d_ff = wfout_ref.shape[0]

    @pl.when(s == 0)
    def _():
        mix_ref[...] = jnp.zeros(mix_ref.shape, mix_ref.dtype)
        wao_b[...] = wao_ref[...].astype(BF16)
        wco_b[...] = wco_ref[...].astype(BF16)
        wout_b[...] = wout_ref[...].astype(BF16)

    conv_tile = jnp.minimum(s, n_tiles - 1)
    first_in_seq = conv_tile % tiles_per_seq == 0
    row0 = jnp.minimum(s, 0)

    hist = jnp.where(first_in_seq, mhalo_ref[...], halo_ref[...]).astype(F32)
    for cb in range(d // LANES):
        csl = slice(cb * LANES, (cb + 1) * LANES)
        buf_ref[cb, 0:CONV_PAD, :] = hist[:, csl]
        buf_ref[cb, CONV_PAD:CONV_PAD + tm, :] = u_ref[:, csl].astype(F32)

    base = CONV_PAD - (CONV_K - 1)
    pieces = [(cb, r0) for cb in range(d // LANES) for r0 in range(0, tm, CONV_RB)]
    n_pieces = len(pieces)
    n_chunks = d_ff // FF_CHUNK
    n_groups = -(-n_chunks // FF_GROUP)
    wide, narrow = d // FF_CHUNK, 1
    budget = PRE_FFN_UNITS + 2 * (n_chunks - 1) * narrow + (n_groups - 1) * wide
    spent = [0]
    yc_c0, hb_c0 = 0, d
    gate_row = pl.multiple_of(row0, 16)

    def gated(c0):
        return mix_ref[pl.ds(gate_row, tm), c0:c0 + d]

    def conv_pieces(cost):
        lo = min(spent[0], budget) * n_pieces // budget
        spent[0] += cost
        hi = min(spent[0], budget) * n_pieces // budget
        for _ in range(hi - lo):
            cb, r0 = pieces.pop(0)
            csl = slice(cb * LANES, (cb + 1) * LANES)
            acc = jnp.broadcast_to(cb_ref[:, csl], (CONV_RB, LANES))
            for res in range(8):
                taps = list(range(res, CONV_K, 8))
                n_rows = CONV_RB + 8 * (len(taps) - 1)
                win = buf_ref[cb, pl.ds(row0 + (r0 + base + res), n_rows), :]
                for t, kk in enumerate(taps):
                    acc = acc + cw_ref[kk:kk + 1, csl] * win[8 * t:8 * t + CONV_RB]
            mix_ref[r0:r0 + CONV_RB, yc_c0 + cb * LANES:yc_c0 + (cb + 1) * LANES] = acc.astype(BF16)

    yc = mix_ref[:, yc_c0:yc_c0 + d].astype(F32)
    mu = jnp.mean(yc, axis=-1, keepdims=True)
    xc = yc - mu
    var = jnp.mean(xc * xc, axis=-1, keepdims=True)
    z = xc * lax.rsqrt(var + EPS) * lng_ref[...] + lnb_ref[...]
    z = (z * _sigmoid(z)).astype(BF16)

    y_a = _dot(o_ref[...], wao_b[...])
    y_b = _dot(z, wco_b[...])
    merged = sga_ref[...].astype(F32) * y_a + sgb_ref[...].astype(F32) * y_b
    mo = _dot(merged.astype(BF16), wout_b[...])
    buf_ref[0, CONV_PAD + tm:, :] = y_b[-CONV_SPARE:, -LANES:]
    conv_pieces(PRE_FFN_UNITS)
    x1 = x_ref[...] + _rms(mo, gpost_ref[...])

    mix_ref[:, hb_c0:hb_c0 + d] = _rms(x1, gfpre_ref[...]).astype(BF16)
    f = None
    for grp in range(n_groups):
        g0 = grp * FF_GROUP * FF_CHUNK
        g1 = min(g0 + FF_GROUP * FF_CHUNK, d_ff)
        for c0 in range(g0, g1, FF_CHUNK):
            if (c0 // FF_CHUNK) % 2 == 0 or c0 == d_ff - FF_CHUNK:
                hb = gated(hb_c0)
            g = _dot(hb, wfin_ref[:, c0:c0 + FF_CHUNK])
            conv_pieces(narrow)
            up = _dot(hb, wfin_ref[:, d_ff + c0:d_ff + c0 + FF_CHUNK])
            conv_pieces(narrow)
            act_ref[:, c0:c0 + FF_CHUNK] = ((g + g * jnp.tanh(g)) * up).astype(BF16)
        part = _dot(act_ref[:, g0:g1], wfout_ref[g0:g1, :])
        conv_pieces(wide)
        f = part if f is None else f + part
    assert not pieces
    out_ref[...] = x1 + _rms(f, gfpost_ref[...])


def _tail(x2d, o2d, sga, sgb, u2d, mhalo, weights, tm, seq):
    rows, d = x2d.shape
    d_ff = weights[10].shape[0]
    n_tiles = rows // tm
    halo_per_tile = tm // CONV_PAD
    lag_spec = pl.BlockSpec((tm, d), lambda s: (jnp.maximum(s - 1, 0), 0))
    conv_tile = lambda s: jnp.minimum(s, n_tiles - 1)
    return pl.pallas_call(
        functools.partial(_tail_kernel, tiles_per_seq=seq // tm, n_tiles=n_tiles),
        grid=(n_tiles + 1,),
        in_specs=[lag_spec] * 4 + [
            pl.BlockSpec((tm, d), lambda s: (conv_tile(s), 0)),
            pl.BlockSpec((CONV_PAD, d),
                         lambda s: (jnp.maximum(conv_tile(s) * halo_per_tile - 1, 0), 0)),
        ] + [_const_spec(mhalo.shape)] + [_const_spec(w.shape) for w in weights],
        out_specs=lag_spec,
        out_shape=jax.ShapeDtypeStruct((rows, d), F32),
        scratch_shapes=[pltpu.VMEM((d // LANES, tm + CONV_PAD + CONV_SPARE, LANES), F32),
                        pltpu.VMEM((tm, 2 * d), BF16),
                        pltpu.VMEM((tm, d_ff), BF16),
                        pltpu.VMEM((d, d), BF16),
                        pltpu.VMEM((d, d), BF16),
                        pltpu.VMEM((d, d), BF16)],
        compiler_params=pltpu.CompilerParams(
            dimension_semantics=("arbitrary",), vmem_limit_bytes=VMEM_LIMIT),
        name="tail",
    )(x2d, o2d, sga, sgb, u2d, u2d, mhalo, *weights)


def _rope_tables(length):
    pos = np.arange(length, dtype=np.float32)
    inv = np.float32(ROPE_BASE) ** (-np.arange(0, QK_ROPE, 2, dtype=np.float32) / QK_ROPE)
    ang = (pos[:, None] * inv[None, :]).astype(np.float32)
    cos, sin = np.cos(ang), np.sin(ang)
    zeros = lambda w: np.zeros((length, w), np.float32)
    ktail = LANES - QK_ROPE
    ck = np.concatenate([cos, cos, zeros(ktail)], axis=1)
    sk_lo = np.concatenate([-sin, zeros(HALF_ROPE + ktail)], axis=1)
    sk_hi = np.concatenate([zeros(HALF_ROPE), sin, zeros(ktail)], axis=1)
    tabs = np.stack([ck, sk_lo, sk_hi]).astype(np.float32)
    tabs_t = (np.stack([cos.T, sin.T]) * np.float32(Q_SCALE)).astype(np.float32)
    return tabs, tabs_t


def kernel(x, meta, mix_pre_g, w_in, q_norm_g, w_uq, kv_norm_g, w_ukv, w_attn_o,
           conv_w, conv_b, conv_ln_g, conv_ln_b, w_conv_o, w_out, mix_post_g,
           ffn_pre_g, w_ffn_in, w_ffn_out, ffn_post_g):
    assert w_in.shape[0] == 1, "single-layer block"
    b, s, d = x.shape
    row = lambda g: g[0][None, :].astype(F32)

    w_in0 = w_in[0]
    n_in = w_in0.shape[1]
    col_scale = jnp.concatenate([jnp.ones((n_in - 3 * d,), F32), jnp.full((3 * d,), 0.5, F32)])
    wint = (w_in0.T * col_scale[:, None]).astype(BF16)

    qk_dim = QK_NOPE + QK_ROPE
    wq = w_uq[0].reshape(Q_RANK, N_HEADS, qk_dim)
    wq = jnp.pad(wq, ((0, 0), (0, 0), (0, HEAD_W - qk_dim)))
    wqt = wq.reshape(Q_RANK, N_HEADS * HEAD_W).T.astype(BF16)

    wkv = w_ukv[0].reshape(KV_RANK, N_HEADS, QK_NOPE + V_DIM)
    wk_nope = jnp.pad(wkv[:, :, :QK_NOPE], ((0, 0), (0, 0), (0, HEAD_W - QK_NOPE)))
    place = np.zeros((LANES, N_HEADS, HEAD_W), np.float32)
    jj = np.arange(QK_ROPE)
    place[jj, :, QK_NOPE + jj] = 1.0
    wk = jnp.concatenate([wk_nope, jnp.asarray(place)], axis=0)
    wk = wk.reshape(KV_RANK + LANES, N_HEADS * HEAD_W).astype(BF16)
    wvt = wkv[:, :, QK_NOPE:].reshape(KV_RANK, N_HEADS * V_DIM).T.astype(BF16)

    in_weights = (row(mix_pre_g), wint, row(q_norm_g), wqt, row(kv_norm_g), wk, wvt)

    tabs, tabs_t = _rope_tables(N_META + s)

    tm = 256
    x2d = x.reshape(b * s, d)
    qt, k, vt, u, sga, sgb = _inproj(x2d, tabs[:, N_META:], tabs_t[:, :, N_META:],
                                     in_weights, 2 * tm, s)
    _, km, vmt, um, _, _ = _inproj(meta.astype(F32), tabs[:, :N_META],
                                   tabs_t[:, :, :N_META], in_weights, N_META, N_META)

    o = _attention(qt, k.reshape(b, s, -1), vt, km, vmt[0], tq=256)

    mhalo = jnp.concatenate([jnp.zeros((CONV_PAD - N_META, d), BF16), um], axis=0)
    d_ff = w_ffn_out.shape[1]
    ffn_scale = jnp.concatenate([jnp.full((d_ff,), 0.5, F32), jnp.ones((d_ff,), F32)])
    tail_weights = (conv_w[0].astype(F32), row(conv_b), row(conv_ln_g), row(conv_ln_b),
                    w_attn_o[0], w_conv_o[0], w_out[0], row(mix_post_g), row(ffn_pre_g),
                    (w_ffn_in[0] * ffn_scale).astype(BF16), w_ffn_out[0].astype(BF16),
                    row(ffn_post_g))
    out = _tail(x2d, o.reshape(b * s, -1), sga, sgb, u, mhalo, tail_weights, tm, s)
    return out.reshape(b, s, d)
```
